```python
import jax, jax.numpy as jnp
from jax import lax
import numpy as np

D_MODEL = 2048
BATCH = 8
SEQ = 2048
DEPTH = 4

N_MIXERS = 2
HEAD_DIM = 128
MIX_WIDTH = D_MODEL
N_MEM_HEADS = 4
N_MIX_HEADS = (MIX_WIDTH - N_MEM_HEADS * HEAD_DIM) // HEAD_DIM
N_KV_GROUPS = 2
HEADS_PER_GROUP = N_MIX_HEADS // N_KV_GROUPS
N_MEM = 256
Q_BLOCK = 128
CMP_LEN = 32
CMP_STRIDE = 16
CMP_HIDDEN = 2 * HEAD_DIM
SEL_BLOCK = 64
N_SEL = 16
WINDOW = 512
N_BRANCH = 3
D_FF = 5632
CONV_WIDTH = 3
ROPE_THETA = 10000.0
NORM_EPS = 1e-6
SB_COLS = 3 * N_MIX_HEADS * HEAD_DIM + N_MEM_HEADS * HEAD_DIM
NSA_COLS = (N_MIX_HEADS * HEAD_DIM + 6 * N_KV_GROUPS * HEAD_DIM
            + N_BRANCH * N_MIX_HEADS + N_MEM_HEADS * HEAD_DIM)

kernel_name = "hybrid_stickbreak_nsa_memory_convffn"


def rmsnorm(x, g):
    xf = x.astype(jnp.float32)
    y = xf * lax.rsqrt(jnp.mean(xf * xf, axis=-1, keepdims=True) + NORM_EPS)
    return (y * g.astype(jnp.float32)).astype(x.dtype)


def rope_tables(positions):
    inv_freq = jnp.power(ROPE_THETA, -jnp.arange(0, HEAD_DIM, 2, dtype=jnp.float32) / HEAD_DIM)
    ang = positions.astype(jnp.float32)[..., None] * inv_freq
    return jnp.cos(ang)[:, :, None, :], jnp.sin(ang)[:, :, None, :]


def apply_rope(u, cos, sin):
    uf = u.astype(jnp.float32)
    u1, u2 = uf[..., :HEAD_DIM // 2], uf[..., HEAD_DIM // 2:]
    return jnp.concatenate([u1 * cos - u2 * sin, u2 * cos + u1 * sin], axis=-1).astype(u.dtype)


def masked_softmax(s, mask):
    s = jnp.where(mask, s, -jnp.inf)
    m = jnp.max(s, axis=-1, keepdims=True)
    m = jnp.where(jnp.isfinite(m), m, 0.0)
    e = jnp.exp(s - m)
    return e / jnp.maximum(jnp.sum(e, axis=-1, keepdims=True), 1e-30)


def split_cols(a, sizes):
    outs, start = [], 0
    for n in sizes:
        outs.append(a[..., start:start + n])
        start += n
    return outs


def stick_breaking_attention(q, k, v):
    B, S, H, Dh = q.shape
    scale = Dh ** -0.5
    outs = []
    for blk in range(S // Q_BLOCK):
        end = (blk + 1) * Q_BLOCK
        qb = q[:, blk * Q_BLOCK:end]
        kp, vp = k[:, :end], v[:, :end]
        z = jnp.einsum('bthd,bshd->bhts', qb, kp).astype(jnp.float32) * scale
        t_pos = blk * Q_BLOCK + jnp.arange(Q_BLOCK)
        mask = jnp.arange(end)[None, :] < t_pos[:, None]
        log_keep = jnp.where(mask, jax.nn.log_sigmoid(-z), 0.0)
        log_keep_between = lax.cumsum(log_keep, axis=3, reverse=True) - log_keep
        a = jnp.where(mask, jnp.exp(jax.nn.log_sigmoid(z) + log_keep_between), 0.0)
        outs.append(jnp.einsum('bhts,bshd->bthd', a.astype(v.dtype), vp))
    return jnp.concatenate(outs, axis=1)


def stick_breaking_mixer(h, w_in):
    B, S, _ = h.shape
    q, k, v, q_mem = split_cols(h @ w_in, [N_MIX_HEADS * HEAD_DIM] * 3 + [N_MEM_HEADS * HEAD_DIM])
    shp = (B, S, N_MIX_HEADS, HEAD_DIM)
    o = stick_breaking_attention(q.reshape(shp), k.reshape(shp), v.reshape(shp))
    return o.reshape(B, S, N_MIX_HEADS * HEAD_DIM), q_mem


def compress(u, pe, w1, w2):
    B, S, G, Dh = u.shape
    ch = u.reshape(B, S // CMP_STRIDE, CMP_STRIDE, G, Dh)
    blk = jnp.concatenate([ch[:, :-1], ch[:, 1:]], axis=2) + pe[:, None, :]
    nc = blk.shape[1]
    flat = blk.transpose(0, 1, 3, 2, 4).reshape(B, nc, G, CMP_LEN * Dh)
    return jax.nn.silu(flat @ w1) @ w2


def overlap_matrix(nc, ns):
    c0 = jnp.arange(nc)[:, None] * CMP_STRIDE
    s0 = jnp.arange(ns)[None, :] * SEL_BLOCK
    ov = jnp.minimum(c0 + CMP_LEN, s0 + SEL_BLOCK) - jnp.maximum(c0, s0)
    return jnp.maximum(ov, 0).astype(jnp.float32) / CMP_STRIDE


def nsa_mixer(h, w_in, pe, w1, w2, cos, sin):
    B, S, _ = h.shape
    G, HPG, Dh = N_KV_GROUPS, HEADS_PER_GROUP, HEAD_DIM
    kvw = G * Dh
    q, kc, vc, ks, vs, kw, vw, g_logit, q_mem = split_cols(
        h @ w_in, [N_MIX_HEADS * Dh] + [kvw] * 6 + [N_BRANCH * N_MIX_HEADS, N_MEM_HEADS * Dh])
    q = apply_rope(q.reshape(B, S, N_MIX_HEADS, Dh), cos, sin)
    kc = apply_rope(kc.reshape(B, S, G, Dh), cos, sin)
    ks = apply_rope(ks.reshape(B, S, G, Dh), cos, sin)
    kw = apply_rope(kw.reshape(B, S, G, Dh), cos, sin)
    vc, vs, vw = (u.reshape(B, S, G, Dh) for u in (vc, vs, vw))
    gates = jax.nn.sigmoid(g_logit.astype(jnp.float32)).reshape(B, S, N_MIX_HEADS, N_BRANCH)
    scale = Dh ** -0.5
    t = jnp.arange(S)
    qg = q.reshape(B, S, G, HPG, Dh)

    k_cmp = compress(kc, pe[0], w1[0], w2[0])
    v_cmp = compress(vc, pe[1], w1[1], w2[1])
    nc = k_cmp.shape[1]
    s_c = jnp.einsum('btghd,bcgd->bghtc', qg, k_cmp).astype(jnp.float32) * scale
    c_end = jnp.arange(nc) * CMP_STRIDE + CMP_LEN - 1
    p_c = masked_softmax(s_c, c_end[None, :] <= t[:, None])
    o_cmp = jnp.einsum('bghtc,bcgd->btghd', p_c.astype(v_cmp.dtype), v_cmp)

    ns = S // SEL_BLOCK
    imp = jnp.einsum('bghtc,cj->bgtj', p_c, overlap_matrix(nc, ns))
    cur = t // SEL_BLOCK
    jb = jnp.arange(ns)[None, :]
    valid = jb <= cur[:, None]
    forced = (jb == 0) | (jb == cur[:, None]) | (jb == cur[:, None] - 1)
    score = jnp.where(forced, jnp.inf, jnp.where(valid, imp, -jnp.inf))
    n_sel = min(N_SEL, ns)
    _, idx = lax.top_k(score, n_sel)
    sel_ok = idx <= cur[None, None, :, None]

    ks_blocks = ks.reshape(B, ns, SEL_BLOCK, G, Dh).transpose(0, 3, 1, 2, 4)
    vs_blocks = vs.reshape(B, ns, SEL_BLOCK, G, Dh).transpose(0, 3, 1, 2, 4)
    pad = ((0, 0), (WINDOW, 0), (0, 0), (0, 0))
    kw_pad, vw_pad = jnp.pad(kw, pad), jnp.pad(vw, pad)
    gather_blocks = jax.vmap(jax.vmap(lambda kb, ib: kb[ib]))

    nqb = S // Q_BLOCK
    q_blocks = jnp.moveaxis(qg.reshape(B, nqb, Q_BLOCK, G, HPG, Dh), 1, 0)
    idx_blocks = jnp.moveaxis(idx.reshape(B, G, nqb, Q_BLOCK, n_sel), 2, 0)
    ok_blocks = jnp.moveaxis(sel_ok.reshape(B, G, nqb, Q_BLOCK, n_sel), 2, 0)

    def query_block(args):
        qb, idxb, okb, bi = args
        tq = bi * Q_BLOCK + jnp.arange(Q_BLOCK)
        kg = gather_blocks(ks_blocks, idxb)
        vg = gather_blocks(vs_blocks, idxb).reshape(B, G, Q_BLOCK, n_sel * SEL_BLOCK, Dh)
        s_s = jnp.einsum('btghd,bgtnkd->bghtnk', qb, kg).astype(jnp.float32) * scale
        s_s = s_s.reshape(B, G, HPG, Q_BLOCK, n_sel * SEL_BLOCK)
        tok = idxb[..., None] * SEL_BLOCK + jnp.arange(SEL_BLOCK)
        m_s = okb[..., None] & (tok <= tq[None, None, :, None, None])
        p_s = masked_softmax(s_s, m_s.reshape(B, G, 1, Q_BLOCK, n_sel * SEL_BLOCK))
        o_s = jnp.einsum('bghtm,bgtmd->btghd', p_s.astype(vg.dtype), vg)
        kwin = lax.dynamic_slice_in_dim(kw_pad, bi * Q_BLOCK, Q_BLOCK + WINDOW, axis=1)
        vwin = lax.dynamic_slice_in_dim(vw_pad, bi * Q_BLOCK, Q_BLOCK + WINDOW, axis=1)
        kp = bi * Q_BLOCK - WINDOW + jnp.arange(Q_BLOCK + WINDOW)
        m_w = (kp[None, :] <= tq[:, None]) & (kp[None, :] > tq[:, None] - WINDOW) & (kp[None, :] >= 0)
        s_w = jnp.einsum('btghd,bsgd->bghts', qb, kwin).astype(jnp.float32) * scale
        p_w = masked_softmax(s_w, m_w)
        o_w = jnp.einsum('bghts,bsgd->btghd', p_w.astype(vwin.dtype), vwin)
        return o_s, o_w

    o_sel, o_win = lax.map(query_block, (q_blocks, idx_blocks, ok_blocks, jnp.arange(nqb)))
    shp = (B, S, N_MIX_HEADS, Dh)
    o_sel = jnp.moveaxis(o_sel, 0, 1).reshape(shp)
    o_win = jnp.moveaxis(o_win, 0, 1).reshape(shp)
    o = (gates[..., 0:1] * o_cmp.reshape(shp) + gates[..., 1:2] * o_sel
         + gates[..., 2:3] * o_win).astype(h.dtype)
    return o.reshape(B, S, N_MIX_HEADS * Dh), q_mem


def memory_attention(q_mem, mem_n, w_kv):
    B, S, _ = q_mem.shape
    km, vm = split_cols(mem_n @ w_kv, [N_MEM_HEADS * HEAD_DIM] * 2)
    q = q_mem.reshape(B, S, N_MEM_HEADS, HEAD_DIM)
    km = km.reshape(B, -1, N_MEM_HEADS, HEAD_DIM)
    vm = vm.reshape(B, -1, N_MEM_HEADS, HEAD_DIM)
    s = jnp.einsum('bthd,bmhd->bhtm', q, km).astype(jnp.float32) * HEAD_DIM ** -0.5
    p = jax.nn.softmax(s, axis=-1)
    return jnp.einsum('bhtm,bmhd->bthd', p.astype(vm.dtype), vm).reshape(B, S, N_MEM_HEADS * HEAD_DIM)


def conv_ffn(h, w_gate, w_up, conv_w, conv_b, w_down):
    S = h.shape[1]
    g_pre = h @ w_gate
    gp = jnp.pad(g_pre, ((0, 0), (CONV_WIDTH - 1, 0), (0, 0)))
    gc = conv_b
    for k in range(CONV_WIDTH):
        gc = gc + gp[:, k:k + S] * conv_w[k]
    return (jax.nn.silu(gc) * (h @ w_up)) @ w_down


def setup_inputs(seed: int = 0) -> dict:
    key = jax.random.key(seed)
    ks = jax.random.split(key, 16)
    f32 = jnp.float32
    n_sb = len(range(0, DEPTH, N_MIXERS))
    n_nsa = len(range(1, DEPTH, N_MIXERS))

    def normal(k, shape, scale):
        return jax.random.normal(k, shape, f32) * scale

    positions = (jax.random.randint(ks[2], (BATCH, 1), 0, 4096, dtype=jnp.int32)
                 + jnp.arange(SEQ, dtype=jnp.int32)[None, :])
    return {
        "x": normal(ks[0], (BATCH, SEQ, D_MODEL), 1.0),
        "mem": normal(ks[1], (BATCH, N_MEM, D_MODEL), 1.0),
        "positions": positions,
        "norm_g": 1.0 + normal(ks[3], (DEPTH, 5, D_MODEL), 0.05),
        "w_in_sb": normal(ks[4], (n_sb, D_MODEL, SB_COLS), D_MODEL ** -0.5),
        "w_in_nsa": normal(ks[5], (n_nsa, D_MODEL, NSA_COLS), D_MODEL ** -0.5),
        "cmp_pe": normal(ks[6], (n_nsa, 2, CMP_LEN, HEAD_DIM), 0.1),
        "cmp_w1": normal(ks[7], (n_nsa, 2, CMP_LEN * HEAD_DIM, CMP_HIDDEN), (CMP_LEN * HEAD_DIM) ** -0.5),
        "cmp_w2": normal(ks[8], (n_nsa, 2, CMP_HIDDEN, HEAD_DIM), CMP_HIDDEN ** -0.5),
        "w_mem_kv": normal(ks[9], (DEPTH, D_MODEL, 2 * N_MEM_HEADS * HEAD_DIM), D_MODEL ** -0.5),
        "w_o": normal(ks[10], (DEPTH, MIX_WIDTH, D_MODEL), MIX_WIDTH ** -0.5),
        "w_ffn_gate": normal(ks[11], (DEPTH, D_MODEL, D_FF), D_MODEL ** -0.5),
        "w_ffn_up": normal(ks[12], (DEPTH, D_MODEL, D_FF), D_MODEL ** -0.5),
        "ffn_conv_w": normal(ks[13], (DEPTH, CONV_WIDTH, D_FF), CONV_WIDTH ** -0.5),
        "ffn_conv_b": normal(ks[14], (DEPTH, D_FF), 0.01),
        "w_ffn_down": normal(ks[15], (DEPTH, D_FF, D_MODEL), D_FF ** -0.5),
    }


def reference(x, mem, positions, norm_g, w_in_sb, w_in_nsa, cmp_pe, cmp_w1, cmp_w2,
              w_mem_kv, w_o, w_ffn_gate, w_ffn_up, ffn_conv_w, ffn_conv_b, w_ffn_down):
    cos, sin = rope_tables(positions)
    for i in range(DEPTH):
        j = i // N_MIXERS
        h = rmsnorm(x, norm_g[i, 0])
        if i % N_MIXERS == 0:
            o_mix, q_mem = stick_breaking_mixer(h, w_in_sb[j])
        else:
            o_mix, q_mem = nsa_mixer(h, w_in_nsa[j], cmp_pe[j], cmp_w1[j], cmp_w2[j], cos, sin)
        o_mem = memory_attention(q_mem, rmsnorm(mem, norm_g[i, 4]), w_mem_kv[i])
        attn = jnp.concatenate([o_mix, o_mem], axis=-1) @ w_o[i]
        x = x + rmsnorm(attn, norm_g[i, 1])
        h = rmsnorm(x, norm_g[i, 2])
        f = conv_ffn(h, w_ffn_gate[i], w_ffn_up[i], ffn_conv_w[i], ffn_conv_b[i], w_ffn_down[i])
        x = x + rmsnorm(f, norm_g[i, 3])
    return x
```

```python
import functools

import jax
import jax.numpy as jnp
from jax import lax
from jax.experimental import pallas as pl
from jax.experimental.pallas import tpu as pltpu

F32 = jnp.float32
BF16 = jnp.bfloat16

HEAD_DIM = 128
N_MIX_HEADS = 12
N_MEM_HEADS = 4
N_KV_GROUPS = 2
HEADS_PER_GROUP = N_MIX_HEADS // N_KV_GROUPS
N_BRANCH = 3
CMP_LEN = 32
CMP_STRIDE = 16
SEL_BLOCK = 64
N_SEL = 16
WINDOW = 512
CONV_WIDTH = 3
ROPE_THETA = 10000.0
NORM_EPS = 1e-6
ATTN_SCALE = HEAD_DIM ** -0.5
MASK_VALUE = -1e30

LANES = 128
MIX_COLS = N_MIX_HEADS * HEAD_DIM
MEM_COLS = N_MEM_HEADS * HEAD_DIM
KV_COLS = N_KV_GROUPS * HEAD_DIM
CONV_HALO = 16

_NT = (((1,), (1,)), ((), ()))


def _cparams(semantics, vmem_mib):
    return pltpu.CompilerParams(dimension_semantics=semantics,
                                vmem_limit_bytes=vmem_mib * 1024 * 1024)


def _dot(a, b):
    return jnp.dot(a, b, preferred_element_type=F32)


def _dot_nt(a, b):
    return lax.dot_general(a, b, _NT, preferred_element_type=F32)


def _split_bf16(x):
    hi = x.astype(BF16)
    lo = (x - hi.astype(F32)).astype(BF16)
    return jnp.concatenate([hi, lo], axis=1)


def _sigmoid(x):
    return 1.0 / (1.0 + jnp.exp(-x))


def _rms_rows(x, g):
    ms = jnp.mean(x * x, axis=-1, keepdims=True)
    return (x * lax.rsqrt(ms + NORM_EPS)) * g


def _rope_kernel(pos_ref, inv_ref, sign_ref, cos_ref, sin_ref):
    ang = pos_ref[...].astype(F32) * inv_ref[...]
    cos_ref[...] = jnp.cos(ang)
    sin_ref[...] = jnp.sin(ang) * sign_ref[...]


def rope_tables(positions):
    m = positions.size
    tm = min(m, 2048)
    inv_freq = jnp.power(ROPE_THETA, -jnp.arange(0, HEAD_DIM, 2, dtype=F32) / HEAD_DIM)
    inv_full = jnp.concatenate([inv_freq, inv_freq])[None, :]
    sign = jnp.concatenate([-jnp.ones((HEAD_DIM // 2,), F32), jnp.ones((HEAD_DIM // 2,), F32)])[None, :]
    row = pl.BlockSpec((tm, LANES), lambda i: (i, 0))
    const = pl.BlockSpec((1, LANES), lambda i: (0, 0))
    return pl.pallas_call(
        _rope_kernel,
        grid=(m // tm,),
        in_specs=[pl.BlockSpec((tm, 1), lambda i: (i, 0)), const, const],
        out_specs=[row, row],
        out_shape=[jax.ShapeDtypeStruct((m, LANES), F32)] * 2,
        compiler_params=_cparams(("parallel",), 32),
        name="rope_tables",
    )(positions.reshape(m, 1), inv_full, sign)


def _norm_matmul_kernel(x_ref, g_ref, w_ref, cos_ref, sin_ref, o_ref, hn_ref, *, rope_tiles, norm_chunk):
    j = pl.program_id(1)
    tm = x_ref.shape[0]

    @pl.when(j == 0)
    def _():
        def body(c, _):
            r = pl.multiple_of(c * norm_chunk, norm_chunk)
            hn_ref[pl.ds(r, norm_chunk), :] = _rms_rows(x_ref[pl.ds(r, norm_chunk), :], g_ref[...]).astype(BF16)
            return 0
        lax.fori_loop(0, tm // norm_chunk, body, 0)

    y = _dot(hn_ref[...], w_ref[...])

    if rope_tiles == 0:
        o_ref[...] = y.astype(o_ref.dtype)
    else:
        @pl.when(j < rope_tiles)
        def _():
            cosf, sinf = cos_ref[...], sin_ref[...]
            for c in range(y.shape[1] // HEAD_DIM):
                u = y[:, c * HEAD_DIM:(c + 1) * HEAD_DIM]
                o_ref[:, c * HEAD_DIM:(c + 1) * HEAD_DIM] = (
                    u * cosf + pltpu.roll(u, HEAD_DIM // 2, 1) * sinf).astype(o_ref.dtype)

        @pl.when(j >= rope_tiles)
        def _():
            o_ref[...] = y.astype(o_ref.dtype)


def norm_matmul(x, g, w, cosf, sinf, *, tm, tn, rope_cols=0):
    m, d = x.shape
    n = w.shape[1]
    assert m % tm == 0 and n % tn == 0 and rope_cols % tn == 0
    kern = functools.partial(_norm_matmul_kernel, rope_tiles=rope_cols // tn, norm_chunk=min(tm, 128))
    return pl.pallas_call(
        kern,
        grid=(m // tm, n // tn),
        in_specs=[
            pl.BlockSpec((tm, d), lambda i, j: (i, 0)),
            pl.BlockSpec((1, d), lambda i, j: (0, 0)),
            pl.BlockSpec((d, tn), lambda i, j: (0, j)),
            pl.BlockSpec((tm, LANES), lambda i, j: (i, 0)),
            pl.BlockSpec((tm, LANES), lambda i, j: (i, 0)),
        ],
        out_specs=pl.BlockSpec((tm, tn), lambda i, j: (i, j)),
        out_shape=jax.ShapeDtypeStruct((m, n), BF16),
        scratch_shapes=[pltpu.VMEM((tm, d), BF16)],
        compiler_params=_cparams(("parallel", "arbitrary"), 48),
        name="norm_matmul",
    )(x, g.reshape(1, d), w, cosf, sinf)


def _sb_kernel(q_ref, k_ref, v_ref, u_ref, o_ref, *, tq):
    qi = pl.program_id(2)
    q = q_ref[...]
    ri = lax.broadcasted_iota(jnp.int32, (tq, tq), 0)
    ci = lax.broadcasted_iota(jnp.int32, (tq, tq), 1)
    causal = ci < ri

    def block(kb, c, diagonal):
        start = pl.multiple_of(kb * tq, tq)
        k = k_ref[pl.ds(start, tq), :]
        v = v_ref[pl.ds(start, tq), :]
        z = _dot_nt(q, k) * ATTN_SCALE
        log_beta = jnp.minimum(z, 0.0) - jnp.log1p(jnp.exp(-jnp.abs(z)))
        log_keep = log_beta - z
        if diagonal:
            log_keep = jnp.where(causal, log_keep, 0.0)
        later = _dot(_split_bf16(log_keep), u_ref[...])
        a = jnp.exp(log_beta + later + c)
        if diagonal:
            a = jnp.where(causal, a, 0.0)
        pv = _dot(a.astype(BF16), v)
        return pv, c + later[:, :1] + log_keep[:, :1]

    acc, c = block(qi, jnp.zeros((tq, 1), F32), True)

    def body(i, carry):
        acc, c = carry
        pv, c = block(qi - i, c, False)
        return acc + pv, c

    acc, _ = lax.fori_loop(1, qi + 1, body, (acc, c))
    o_ref[...] = acc.astype(o_ref.dtype)


def sb_attention(proj, batch, seq, *, tq):
    nq = seq // tq
    h = N_MIX_HEADS
    j = lax.broadcasted_iota(jnp.int32, (tq, tq), 0)
    s = lax.broadcasted_iota(jnp.int32, (tq, tq), 1)
    upper = (j > s).astype(BF16)
    upper2 = jnp.concatenate([upper, upper], axis=0)
    return pl.pallas_call(
        functools.partial(_sb_kernel, tq=tq),
        grid=(batch, h, nq),
        in_specs=[
            pl.BlockSpec((tq, HEAD_DIM), lambda b, hh, i: (b * nq + i, hh)),
            pl.BlockSpec((seq, HEAD_DIM), lambda b, hh, i: (b, h + hh)),
            pl.BlockSpec((seq, HEAD_DIM), lambda b, hh, i: (b, 2 * h + hh)),
            pl.BlockSpec((2 * tq, tq), lambda b, hh, i: (0, 0)),
        ],
        out_specs=pl.BlockSpec((tq, HEAD_DIM), lambda b, hh, i: (b * nq + i, hh)),
        out_shape=jax.ShapeDtypeStruct((batch * seq, MIX_COLS), BF16),
        compiler_params=_cparams(("parallel", "parallel", "arbitrary"), 32),
        name="sb_attention",
    )(proj, proj, proj, upper2)


def _cmp_kernel(x_ref, pe_ref, w1_ref, w2_ref, o_ref):
    half = CMP_STRIDE * HEAD_DIM
    x = x_ref[...].astype(F32)
    pe = pe_ref[...]
    first = _dot((x + pe[0:1, :]).astype(BF16), w1_ref[:half, :])
    second = _dot((x + pe[1:2, :]).astype(BF16), w1_ref[half:, :])
    n = x.shape[0]
    hid = first + pltpu.roll(second, n - 1, 0)
    hid = hid * _sigmoid(hid)
    o_ref[...] = _dot(hid.astype(BF16), w2_ref[...]).astype(o_ref.dtype)


def cmp_tokens(chunks, pe, w1, w2):
    _, b, g, n, half = chunks.shape
    hid = w1.shape[-1]
    return pl.pallas_call(
        _cmp_kernel,
        grid=(2, b, g),
        in_specs=[
            pl.BlockSpec((None, None, None, n, half), lambda kv, bb, gg: (kv, bb, gg, 0, 0)),
            pl.BlockSpec((None, 2, half), lambda kv, bb, gg: (kv, 0, 0)),
            pl.BlockSpec((None, 2 * half, hid), lambda kv, bb, gg: (kv, 0, 0)),
            pl.BlockSpec((None, hid, HEAD_DIM), lambda kv, bb, gg: (kv, 0, 0)),
        ],
        out_specs=pl.BlockSpec((None, None, None, n, HEAD_DIM), lambda kv, bb, gg: (kv, bb, gg, 0, 0)),
        out_shape=jax.ShapeDtypeStruct((2, b, g, n, HEAD_DIM), BF16),
        compiler_params=_cparams(("parallel", "parallel", "parallel"), 32),
        name="cmp_tokens",
    )(chunks, pe, w1, w2)


def _nsa_kernel(q_ref, kc_ref, vc_ref, ks_ref, vs_ref, kw_ref, vw_ref, gl_ref, ov_ref, ex_ref,
                o_ref, ocmp_ref, selx_ref, *, tq, n_cmp, n_sel_blocks):
    qi = pl.program_id(2)
    hpg = HEADS_PER_GROUP
    t0 = qi * tq
    row = lax.broadcasted_iota(jnp.int32, (tq, LANES), 0) + t0
    lane = lax.broadcasted_iota(jnp.int32, (tq, LANES), 1)

    kc, vc = kc_ref[...], vc_ref[...]
    c_ok = (lane * CMP_STRIDE + (CMP_LEN - 1) <= row) & (lane < n_cmp)
    p_sum = jnp.zeros((tq, LANES), F32)
    for h in range(hpg):
        s = _dot_nt(q_ref[:, h * HEAD_DIM:(h + 1) * HEAD_DIM], kc) * ATTN_SCALE
        m = jnp.max(jnp.where(c_ok, s, MASK_VALUE), axis=-1, keepdims=True)
        e = jnp.where(c_ok, jnp.exp(s - m), 0.0)
        p = e / jnp.maximum(jnp.sum(e, axis=-1, keepdims=True), 1e-30)
        ocmp_ref[:, h * HEAD_DIM:(h + 1) * HEAD_DIM] = _dot(p.astype(BF16), vc)
        p_sum = p_sum + p

    imp = _dot(_split_bf16(p_sum), ov_ref[...])
    cur = jnp.right_shift(row, SEL_BLOCK.bit_length() - 1)
    valid = lane <= cur
    forced = (lane == 0) | (lane == cur) | (lane == cur - 1)
    score = jnp.where(forced, jnp.inf, jnp.where(valid, imp, -jnp.inf))
    rank = jnp.zeros((tq, LANES), jnp.int32)
    for jp in range(n_sel_blocks):
        col = score[:, jp:jp + 1]
        ahead = (col > score) | ((col == score) & (lane > jp))
        rank = rank + jnp.where(ahead, 1, 0)
    chosen = jnp.where((rank < N_SEL) & valid, 1.0, 0.0)
    selx_ref[...] = _dot(chosen.astype(BF16), ex_ref[...])

    qs = jnp.concatenate([q_ref[:, h * HEAD_DIM:(h + 1) * HEAD_DIM] for h in range(hpg)], axis=0)
    kcol = lax.broadcasted_iota(jnp.int32, (tq, tq), 1)
    qrow = lax.broadcasted_iota(jnp.int32, (tq, tq), 0) + t0

    def attend(k_ref, v_ref, kb_lo, allowed):
        def body(kb, carry):
            m, l, acc = carry
            start = pl.multiple_of(kb * tq, tq)
            k = k_ref[pl.ds(start, tq), :]
            v = v_ref[pl.ds(start, tq), :]
            bias = jnp.where(allowed(start, kcol + start), 0.0, MASK_VALUE)
            s = _dot_nt(qs, k) * ATTN_SCALE + jnp.concatenate([bias] * hpg, axis=0)
            m_new = jnp.maximum(m, jnp.max(s, axis=-1, keepdims=True))
            p = jnp.exp(s - m_new)
            alpha = jnp.exp(m - m_new)
            l = alpha * l + jnp.sum(p, axis=-1, keepdims=True)
            acc = alpha * acc + _dot(p.astype(BF16), v)
            return m_new, l, acc
        init = (jnp.full((hpg * tq, 1), MASK_VALUE, F32), jnp.zeros((hpg * tq, 1), F32),
                jnp.zeros((hpg * tq, HEAD_DIM), F32))
        _, l, acc = lax.fori_loop(kb_lo, qi + 1, body, init)
        return acc / l

    def sel_allowed(start, ktok):
        return (selx_ref[:, pl.ds(start, tq)] > 0.5) & (ktok <= qrow)

    def win_allowed(start, ktok):
        return (ktok <= qrow) & (ktok > qrow - WINDOW)

    o_sel = attend(ks_ref, vs_ref, 0, sel_allowed)
    o_win = attend(kw_ref, vw_ref, jnp.maximum(qi - WINDOW // tq, 0), win_allowed)

    gate = _sigmoid(gl_ref[...].astype(F32))
    for h in range(hpg):
        rows = slice(h * tq, (h + 1) * tq)
        cols = slice(h * HEAD_DIM, (h + 1) * HEAD_DIM)
        o = (gate[:, 3 * h:3 * h + 1] * ocmp_ref[:, cols]
             + gate[:, 3 * h + 1:3 * h + 2] * o_sel[rows, :]
             + gate[:, 3 * h + 2:3 * h + 3] * o_win[rows, :])
        o_ref[:, cols] = o.astype(o_ref.dtype)


_NSA_KC, _NSA_KS, _NSA_KW = 12, 14, 16
_NSA_VC, _NSA_VS, _NSA_VW = 18, 20, 22
_NSA_QMEM_COL = 24 * HEAD_DIM
_NSA_GATE = 28
NSA_PROJ_COLS = 30 * HEAD_DIM
NSA_ROPE_COLS = 18 * HEAD_DIM


def nsa_attention(proj, cmp_kv, batch, seq, *, tq):
    nq = seq // tq
    n_chunks = seq // CMP_STRIDE
    n_cmp = n_chunks - 1
    ns = seq // SEL_BLOCK
    assert n_chunks == LANES and ns <= LANES
    c0 = jnp.arange(LANES)[:, None] * CMP_STRIDE
    s0 = jnp.arange(LANES)[None, :] * SEL_BLOCK
    ov = jnp.maximum(jnp.minimum(c0 + CMP_LEN, s0 + SEL_BLOCK) - jnp.maximum(c0, s0), 0)
    ov = (ov.astype(F32) / CMP_STRIDE).astype(BF16)
    ov2 = jnp.concatenate([ov, ov], axis=0)
    expand = (jnp.arange(seq)[None, :] // SEL_BLOCK == jnp.arange(LANES)[:, None]).astype(BF16)
    g_ = N_KV_GROUPS

    def kv_spec(col0):
        return pl.BlockSpec((seq, HEAD_DIM), lambda b, g, i: (b, col0 + g))

    def cmp_spec(which):
        return pl.BlockSpec((None, None, None, n_chunks, HEAD_DIM), lambda b, g, i: (which, b, g, 0, 0))

    return pl.pallas_call(
        functools.partial(_nsa_kernel, tq=tq, n_cmp=n_cmp, n_sel_blocks=ns),
        grid=(batch, g_, nq),
        in_specs=[
            pl.BlockSpec((tq, HEADS_PER_GROUP * HEAD_DIM), lambda b, g, i: (b * nq + i, g)),
            cmp_spec(0), cmp_spec(1),
            kv_spec(_NSA_KS), kv_spec(_NSA_VS), kv_spec(_NSA_KW), kv_spec(_NSA_VW),
            pl.BlockSpec((tq, LANES), lambda b, g, i: (b * nq + i, _NSA_GATE + g)),
            pl.BlockSpec((2 * LANES, LANES), lambda b, g, i: (0, 0)),
            pl.BlockSpec((LANES, seq), lambda b, g, i: (0, 0)),
        ],
        out_specs=pl.BlockSpec((tq, HEADS_PER_GROUP * HEAD_DIM), lambda b, g, i: (b * nq + i, g)),
        out_shape=jax.ShapeDtypeStruct((batch * seq, MIX_COLS), BF16),
        scratch_shapes=[pltpu.VMEM((tq, HEADS_PER_GROUP * HEAD_DIM), F32),
                        pltpu.VMEM((tq, seq), F32)],
        compiler_params=_cparams(("parallel", "parallel", "arbitrary"), 48),
        name="nsa_attention",
    )(proj, cmp_kv, cmp_kv, proj, proj, proj, proj, proj, ov2, expand)


def _mem_kernel(q_ref, kv_ref, o_ref):
    for h in range(N_MEM_HEADS):
        cols = slice(h * HEAD_DIM, (h + 1) * HEAD_DIM)
        k = kv_ref[:, cols]
        v = kv_ref[:, MEM_COLS + h * HEAD_DIM:MEM_COLS + (h + 1) * HEAD_DIM]
        s = _dot_nt(q_ref[:, cols], k) * ATTN_SCALE
        e = jnp.exp(s - jnp.max(s, axis=-1, keepdims=True))
        p = e / jnp.sum(e, axis=-1, keepdims=True)
        o_ref[:, cols] = _dot(p.astype(BF16), v).astype(o_ref.dtype)


def mem_attention(proj, qmem_col, mem_kv, batch, seq, n_mem, *, tq):
    nq = seq // tq
    qblk = qmem_col // MEM_COLS
    assert qmem_col % MEM_COLS == 0
    return pl.pallas_call(
        _mem_kernel,
        grid=(batch, nq),
        in_specs=[
            pl.BlockSpec((tq, MEM_COLS), lambda b, i: (b * nq + i, qblk)),
            pl.BlockSpec((n_mem, 2 * MEM_COLS), lambda b, i: (b, 0)),
        ],
        out_specs=pl.BlockSpec((tq, MEM_COLS), lambda b, i: (b * nq + i, 0)),
        out_shape=jax.ShapeDtypeStruct((batch * seq, MEM_COLS), BF16),
        compiler_params=_cparams(("parallel", "arbitrary"), 32),
        name="mem_attention",
    )(proj, mem_kv)


def _matmul_norm_res_kernel(a_ref, w_ref, g_ref, x_ref, o_ref, acc_ref):
    k = pl.program_id(1)

    @pl.when(k == 0)
    def _():
        acc_ref[...] = jnp.zeros_like(acc_ref)

    acc_ref[...] += _dot(a_ref[...], w_ref[...])

    @pl.when(k == pl.num_programs(1) - 1)
    def _():
        o_ref[...] = x_ref[...] + _rms_rows(acc_ref[...], g_ref[...])


def matmul_norm_residual(a, w, g, x, *, tm, tk):
    m, kdim = a.shape
    d = w.shape[1]
    assert m % tm == 0 and kdim % tk == 0
    return pl.pallas_call(
        _matmul_norm_res_kernel,
        grid=(m // tm, kdim // tk),
        in_specs=[
            pl.BlockSpec((tm, tk), lambda i, k: (i, k)),
            pl.BlockSpec((tk, d), lambda i, k: (k, 0)),
            pl.BlockSpec((1, d), lambda i, k: (0, 0)),
            pl.BlockSpec((tm, d), lambda i, k: (i, 0)),
        ],
        out_specs=pl.BlockSpec((tm, d), lambda i, k: (i, 0)),
        out_shape=jax.ShapeDtypeStruct((m, d), F32),
        scratch_shapes=[pltpu.VMEM((tm, d), F32)],
        compiler_params=_cparams(("parallel", "arbitrary"), 48),
        name="matmul_norm_residual",
    )(a, w, g.reshape(1, d), x)


def _ffn_up_kernel(x_ref, halo_ref, g_ref, wg_ref, wu_ref, cw_ref, cb_ref, o_ref, hn_ref, *, seq, norm_chunk):
    i, j = pl.program_id(0), pl.program_id(1)
    tm = x_ref.shape[0]

    @pl.when(j == 0)
    def _():
        def body(c, _):
            r = pl.multiple_of(c * norm_chunk, norm_chunk)
            hn_ref[pl.ds(r, norm_chunk), :] = _rms_rows(x_ref[pl.ds(r, norm_chunk), :], g_ref[...]).astype(BF16)
            return 0
        lax.fori_loop(0, tm // norm_chunk, body, 0)
        keep = jnp.where((i * tm) % seq == 0, 0.0, 1.0)
        hn_ref[tm:, :] = (_rms_rows(halo_ref[...], g_ref[...]) * keep).astype(BF16)

    hn = hn_ref[...]
    gate = _dot(hn, wg_ref[...])
    up = _dot(hn[:tm, :], wu_ref[...])
    g1 = pltpu.roll(gate, 1, 0)[:tm, :]
    g2 = pltpu.roll(gate, 2, 0)[:tm, :]
    gc = cb_ref[...] + g2 * cw_ref[0:1, :] + g1 * cw_ref[1:2, :] + gate[:tm, :] * cw_ref[2:3, :]
    o_ref[...] = ((gc * _sigmoid(gc)) * up).astype(o_ref.dtype)


def ffn_up(x, g, w_gate, w_up, conv_w, conv_b, seq, *, tm, tf):
    m, d = x.shape
    f = w_gate.shape[1]
    assert m % tm == 0 and f % tf == 0 and seq % tm == 0 and tm % CONV_HALO == 0
    halo_blocks = tm // CONV_HALO
    kern = functools.partial(_ffn_up_kernel, seq=seq, norm_chunk=min(tm, 128))
    return pl.pallas_call(
        kern,
        grid=(m // tm, f // tf),
        in_specs=[
            pl.BlockSpec((tm, d), lambda i, j: (i, 0)),
            pl.BlockSpec((CONV_HALO, d), lambda i, j: (jnp.maximum(i * halo_blocks - 1, 0), 0)),
            pl.BlockSpec((1, d), lambda i, j: (0, 0)),
            pl.BlockSpec((d, tf), lambda i, j: (0, j)),
            pl.BlockSpec((d, tf), lambda i, j: (0, j)),
            pl.BlockSpec((CONV_WIDTH, tf), lambda i, j: (0, j)),
            pl.BlockSpec((1, tf), lambda i, j: (0, j)),
        ],
        out_specs=pl.BlockSpec((tm, tf), lambda i, j: (i, j)),
        out_shape=jax.ShapeDtypeStruct((m, f), BF16),
        scratch_shapes=[pltpu.VMEM((tm + CONV_HALO, d), BF16)],
        compiler_params=_cparams(("parallel", "arbitrary"), 48),
        name="ffn_up",
    )(x, x, g.reshape(1, d), w_gate, w_up, conv_w, conv_b.reshape(1, f))


def _repack_nsa_weight(w):
    q, kc, vc, ks, vs, kw, vw, gl, qm = jnp.split(
        w, [MIX_COLS + KV_COLS * n for n in range(7)] + [MIX_COLS + 6 * KV_COLS + N_BRANCH * N_MIX_HEADS], axis=1)
    per_group = N_BRANCH * HEADS_PER_GROUP
    pad = jnp.zeros((w.shape[0], LANES - per_group), w.dtype)
    gates = [jnp.concatenate([gl[:, g * per_group:(g + 1) * per_group], pad], axis=1) for g in range(N_KV_GROUPS)]
    return jnp.concatenate([q, kc, ks, kw, vc, vs, vw, qm] + gates, axis=1)


def _cmp_chunks(proj, col_block, batch, seq):
    u = proj[:, col_block * HEAD_DIM:(col_block + N_KV_GROUPS) * HEAD_DIM]
    u = u.reshape(batch, seq // CMP_STRIDE, CMP_STRIDE, N_KV_GROUPS, HEAD_DIM)
    return u.transpose(0, 3, 1, 2, 4).reshape(batch, N_KV_GROUPS, seq // CMP_STRIDE, CMP_STRIDE * HEAD_DIM)


def kernel(x, mem, positions, norm_g, w_in_sb, w_in_nsa, cmp_pe, cmp_w1, cmp_w2, w_mem_kv, w_o,
           w_ffn_gate, w_ffn_up, ffn_conv_w, ffn_conv_b, w_ffn_down):
    batch, seq, d = x.shape
    n_mem = mem.shape[1]
    depth = norm_g.shape[0]
    m = batch * seq
    cosf, sinf = rope_tables(positions)
    xs = x.reshape(m, d)
    mem2 = mem.reshape(batch * n_mem, d)
    tm = 512

    for i in range(depth):
        j = i // 2
        if i % 2 == 0:
            proj = norm_matmul(xs, norm_g[i, 0], w_in_sb[j].astype(BF16), cosf, sinf, tm=tm, tn=1024)
            o_mix = sb_attention(proj, batch, seq, tq=256)
            qmem_col = 3 * MIX_COLS
        else:
            proj = norm_matmul(xs, norm_g[i, 0], _repack_nsa_weight(w_in_nsa[j]).astype(BF16), cosf, sinf,
                               tm=tm, tn=768, rope_cols=NSA_ROPE_COLS)
            chunks = jnp.stack([_cmp_chunks(proj, _NSA_KC, batch, seq), _cmp_chunks(proj, _NSA_VC, batch, seq)])
            cmp_kv = cmp_tokens(chunks, cmp_pe[j].reshape(2, 2, CMP_STRIDE * HEAD_DIM),
                                cmp_w1[j].astype(BF16), cmp_w2[j].astype(BF16))
            o_mix = nsa_attention(proj, cmp_kv, batch, seq, tq=128)
            qmem_col = _NSA_QMEM_COL
        mem_kv = norm_matmul(mem2, norm_g[i, 4], w_mem_kv[i].astype(BF16), cosf, sinf, tm=n_mem, tn=1024)
        o_mem = mem_attention(proj, qmem_col, mem_kv, batch, seq, n_mem, tq=512)
        attn_in = jnp.concatenate([o_mix, o_mem], axis=1)
        xs = matmul_norm_residual(attn_in, w_o[i].astype(BF16), norm_g[i, 1], xs, tm=tm, tk=1024)
        act = ffn_up(xs, norm_g[i, 2], w_ffn_gate[i].astype(BF16), w_ffn_up[i].astype(BF16),
                     ffn_conv_w[i], ffn_conv_b[i], seq, tm=tm, tf=512)
        xs = matmul_norm_residual(act, w_ffn_down[i].astype(BF16), norm_g[i, 3], xs, tm=tm, tk=512)
    return xs.reshape(batch, seq, d)
```

```python
import functools
import math

import jax
import jax.numpy as jnp
from jax import lax
from jax.experimental import pallas as pl
from jax.experimental.pallas import tpu as pltpu

F32 = jnp.float32
BF16 = jnp.bfloat16

HEAD_DIM = 128
N_MIX_HEADS = 12
N_MEM_HEADS = 4
N_KV_GROUPS = 2
HEADS_PER_GROUP = N_MIX_HEADS // N_KV_GROUPS
N_BRANCH = 3
CMP_LEN = 32
CMP_STRIDE = 16
SEL_BLOCK = 64
N_SEL = 16
WINDOW = 512
CONV_WIDTH = 3
ROPE_THETA = 10000.0
NORM_EPS = 1e-6
ATTN_SCALE = HEAD_DIM ** -0.5
EXP2_SCALE = ATTN_SCALE * math.log2(math.e)
MASK_VALUE = -1e30

LANES = 128
MIX_COLS = N_MIX_HEADS * HEAD_DIM
MEM_COLS = N_MEM_HEADS * HEAD_DIM
KV_COLS = N_KV_GROUPS * HEAD_DIM
CONV_HALO = 16

_NT = (((1,), (1,)), ((), ()))


def _cparams(semantics, vmem_mib):
    return pltpu.CompilerParams(dimension_semantics=semantics,
                                vmem_limit_bytes=vmem_mib * 1024 * 1024)


def _dot(a, b):
    return jnp.dot(a, b, preferred_element_type=F32)


def _dot_nt(a, b):
    return lax.dot_general(a, b, _NT, preferred_element_type=F32)


def _split_bf16(x, axis):
    hi = x.astype(BF16)
    lo = (x - hi.astype(F32)).astype(BF16)
    return jnp.concatenate([hi, lo], axis=axis)


def _sigmoid(x):
    return 1.0 / (1.0 + jnp.exp(-x))


def _rms_rows(x, g):
    ms = jnp.mean(x * x, axis=-1, keepdims=True)
    return (x * lax.rsqrt(ms + NORM_EPS)) * g


def _rope_kernel(pos_ref, inv_ref, sign_ref, cos_ref, sin_ref):
    ang = pos_ref[...].astype(F32) * inv_ref[...]
    cos_ref[...] = jnp.cos(ang)
    sin_ref[...] = jnp.sin(ang) * sign_ref[...]


def rope_tables(positions):
    m = positions.size
    tm = min(m, 2048)
    inv_freq = jnp.power(ROPE_THETA, -jnp.arange(0, HEAD_DIM, 2, dtype=F32) / HEAD_DIM)
    inv_full = jnp.concatenate([inv_freq, inv_freq])[None, :]
    sign = jnp.concatenate([-jnp.ones((HEAD_DIM // 2,), F32), jnp.ones((HEAD_DIM // 2,), F32)])[None, :]
    row = pl.BlockSpec((tm, LANES), lambda i: (i, 0))
    const = pl.BlockSpec((1, LANES), lambda i: (0, 0))
    return pl.pallas_call(
        _rope_kernel,
        grid=(m // tm,),
        in_specs=[pl.BlockSpec((tm, 1), lambda i: (i, 0)), const, const],
        out_specs=[row, row],
        out_shape=[jax.ShapeDtypeStruct((m, LANES), F32)] * 2,
        compiler_params=_cparams(("parallel",), 32),
        name="rope_tables",
    )(positions.reshape(m, 1), inv_full, sign)


def _norm_matmul_kernel(x_ref, g_ref, w_ref, cos_ref, sin_ref, o_ref, hn_ref, *, rope_tiles, norm_chunk):
    j = pl.program_id(1)
    tm = x_ref.shape[0]

    @pl.when(j == 0)
    def _():
        def body(c, _):
            r = pl.multiple_of(c * norm_chunk, norm_chunk)
            hn_ref[pl.ds(r, norm_chunk), :] = _rms_rows(x_ref[pl.ds(r, norm_chunk), :], g_ref[...]).astype(BF16)
            return 0
        lax.fori_loop(0, tm // norm_chunk, body, 0)

    y = _dot(hn_ref[...], w_ref[...])

    if rope_tiles == 0:
        o_ref[...] = y.astype(o_ref.dtype)
    else:
        @pl.when(j < rope_tiles)
        def _():
            cosf, sinf = cos_ref[...], sin_ref[...]
            for c in range(y.shape[1] // HEAD_DIM):
                u = y[:, c * HEAD_DIM:(c + 1) * HEAD_DIM]
                o_ref[:, c * HEAD_DIM:(c + 1) * HEAD_DIM] = (
                    u * cosf + pltpu.roll(u, HEAD_DIM // 2, 1) * sinf).astype(o_ref.dtype)

        @pl.when(j >= rope_tiles)
        def _():
            o_ref[...] = y.astype(o_ref.dtype)


def norm_matmul(x, g, w, cosf, sinf, *, tm, tn, rope_cols=0):
    m, d = x.shape
    n = w.shape[1]
    assert m % tm == 0 and n % tn == 0 and rope_cols % tn == 0
    kern = functools.partial(_norm_matmul_kernel, rope_tiles=rope_cols // tn, norm_chunk=min(tm, 128))
    return pl.pallas_call(
        kern,
        grid=(m // tm, n // tn),
        in_specs=[
            pl.BlockSpec((tm, d), lambda i, j: (i, 0)),
            pl.BlockSpec((1, d), lambda i, j: (0, 0)),
            pl.BlockSpec((d, tn), lambda i, j: (0, j)),
            pl.BlockSpec((tm, LANES), lambda i, j: (i, 0)),
            pl.BlockSpec((tm, LANES), lambda i, j: (i, 0)),
        ],
        out_specs=pl.BlockSpec((tm, tn), lambda i, j: (i, j)),
        out_shape=jax.ShapeDtypeStruct((m, n), BF16),
        scratch_shapes=[pltpu.VMEM((tm, d), BF16)],
        compiler_params=_cparams(("parallel", "arbitrary"), 48),
        name="norm_matmul",
    )(x, g.reshape(1, d), w, cosf, sinf)


def _sb_kernel(q_ref, k_ref, vt_ref, l_ref, o_ref, *, tq, heads):
    qi = pl.program_id(2)
    key = lax.broadcasted_iota(jnp.int32, (tq, tq), 0)
    qry = lax.broadcasted_iota(jnp.int32, (tq, tq), 1)
    causal = key < qry
    hs = range(heads)
    cols = [slice(h * HEAD_DIM, (h + 1) * HEAD_DIM) for h in hs]

    def blocks(kb, cs, diagonal):
        start = pl.multiple_of(kb * tq, tq)
        raw = [_dot_nt(k_ref[pl.ds(start, tq), cols[h]], q_ref[:, cols[h]]) for h in hs]
        z = [raw[h] * ATTN_SCALE for h in hs]
        e = [jnp.exp2(jnp.abs(raw[h]) * (-EXP2_SCALE)) for h in hs]
        log_beta = [jnp.minimum(z[h], 0.0) - jnp.log(1.0 + e[h]) for h in hs]
        log_keep = [log_beta[h] - z[h] for h in hs]
        if diagonal:
            log_keep = [jnp.where(causal, log_keep[h], 0.0) for h in hs]
        later = [_dot(l_ref[...], _split_bf16(log_keep[h], 0)) for h in hs]
        a = [jnp.exp(log_beta[h] + later[h] + cs[h]) for h in hs]
        if diagonal:
            a = [jnp.where(causal, a[h], 0.0) for h in hs]
        pv = [_dot(vt_ref[h, :, pl.ds(start, tq)], a[h].astype(BF16)) for h in hs]
        return pv, [cs[h] + later[h][0:1, :] + log_keep[h][0:1, :] for h in hs]

    acc, cs = blocks(qi, [jnp.zeros((1, tq), F32)] * heads, True)

    def body(i, carry):
        acc, cs = carry
        pv, cs = blocks(qi - i, cs, False)
        return [acc[h] + pv[h] for h in hs], cs

    acc, _ = lax.fori_loop(1, qi + 1, body, (acc, cs))
    for h in hs:
        o_ref[:, cols[h]] = acc[h].T.astype(o_ref.dtype)


def sb_attention(proj, batch, seq, *, tq, heads):
    nq = seq // tq
    hb = N_MIX_HEADS // heads
    assert N_MIX_HEADS % heads == 0
    width = heads * HEAD_DIM
    v_t = proj[:, 2 * MIX_COLS:3 * MIX_COLS].reshape(batch, seq, N_MIX_HEADS, HEAD_DIM).transpose(0, 2, 3, 1)
    s = lax.broadcasted_iota(jnp.int32, (tq, tq), 0)
    j = lax.broadcasted_iota(jnp.int32, (tq, tq), 1)
    later = (j > s).astype(BF16)
    later2 = jnp.concatenate([later, later], axis=1)
    return pl.pallas_call(
        functools.partial(_sb_kernel, tq=tq, heads=heads),
        grid=(batch, hb, nq),
        in_specs=[
            pl.BlockSpec((tq, width), lambda b, hh, i: (b * nq + i, hh)),
            pl.BlockSpec((seq, width), lambda b, hh, i: (b, hb + hh)),
            pl.BlockSpec((None, heads, HEAD_DIM, seq), lambda b, hh, i: (b, hh, 0, 0)),
            pl.BlockSpec((tq, 2 * tq), lambda b, hh, i: (0, 0)),
        ],
        out_specs=pl.BlockSpec((tq, width), lambda b, hh, i: (b * nq + i, hh)),
        out_shape=jax.ShapeDtypeStruct((batch * seq, MIX_COLS), BF16),
        compiler_params=_cparams(("parallel", "parallel", "arbitrary"), 32),
        name="sb_attention",
    )(proj, proj, v_t, later2)


def _cmp_kernel(x_ref, pe_ref, w1_ref, w2_ref, o_ref):
    half = CMP_STRIDE * HEAD_DIM
    x = x_ref[...].astype(F32)
    pe = pe_ref[...]
    first = _dot((x + pe[0:1, :]).astype(BF16), w1_ref[:half, :])
    second = _dot((x + pe[1:2, :]).astype(BF16), w1_ref[half:, :])
    n = x.shape[0]
    hid = first + pltpu.roll(second, n - 1, 0)
    hid = hid * _sigmoid(hid)
    o_ref[...] = _dot(hid.astype(BF16), w2_ref[...]).astype(o_ref.dtype)


def cmp_tokens(chunks, pe, w1, w2):
    _, b, g, n, half = chunks.shape
    hid = w1.shape[-1]
    return pl.pallas_call(
        _cmp_kernel,
        grid=(2, b, g),
        in_specs=[
            pl.BlockSpec((None, None, None, n, half), lambda kv, bb, gg: (kv, bb, gg, 0, 0)),
            pl.BlockSpec((None, 2, half), lambda kv, bb, gg: (kv, 0, 0)),
            pl.BlockSpec((None, 2 * half, hid), lambda kv, bb, gg: (kv, 0, 0)),
            pl.BlockSpec((None, hid, HEAD_DIM), lambda kv, bb, gg: (kv, 0, 0)),
        ],
        out_specs=pl.BlockSpec((None, None, None, n, HEAD_DIM), lambda kv, bb, gg: (kv, bb, gg, 0, 0)),
        out_shape=jax.ShapeDtypeStruct((2, b, g, n, HEAD_DIM), BF16),
        compiler_params=_cparams(("parallel", "parallel", "parallel"), 32),
        name="cmp_tokens",
    )(chunks, pe, w1, w2)


def _nsa_kernel(q_ref, kc_ref, vct_ref, ks_ref, vst_ref, kw_ref, vwt_ref, glt_ref, ovt_ref,
                o_ref, ocmp_ref, selb_ref, *, tq, n_cmp, n_sel_blocks, chunk):
    qi = pl.program_id(2)
    hpg = HEADS_PER_GROUP
    t0 = qi * tq
    tok = lax.broadcasted_iota(jnp.int32, (LANES, tq), 1) + t0
    sub = lax.broadcasted_iota(jnp.int32, (LANES, tq), 0)
    qs =jnp.concatenate([q_ref[:, h * HEAD_DIM:(h + 1) * HEAD_DIM] for h in range(hpg)], axis=0)

    def per_head(x):
        return jnp.concatenate([x] * hpg, axis=1)

    c_ok = (sub * CMP_STRIDE + (CMP_LEN - 1) <= tok) & (sub < n_cmp)
    s = _dot_nt(kc_ref[...], qs) * ATTN_SCALE + per_head(jnp.where(c_ok, 0.0, MASK_VALUE))
    e = jnp.exp(s - jnp.max(s, axis=0, keepdims=True))
    tok_row = lax.broadcasted_iota(jnp.int32, (1, tq), 1) + t0
    has_block = per_head(jnp.where(tok_row >= CMP_LEN - 1, 1.0, 0.0))
    p = e * (has_block / jnp.sum(e, axis=0, keepdims=True))
    ocmp_ref[...] = _dot(vct_ref[...], p.astype(BF16))
    p_sum = p[:, 0:tq]
    for h in range(1, hpg):
        p_sum = p_sum + p[:, h * tq:(h + 1) * tq]

    imp = _dot(ovt_ref[...], _split_bf16(p_sum, 0))
    nsb = n_sel_blocks
    jrow = lax.broadcasted_iota(jnp.int32, (nsb, tq), 0)
    jtok = lax.broadcasted_iota(jnp.int32, (nsb, tq), 1) + t0
    cur = jnp.right_shift(jtok, SEL_BLOCK.bit_length() - 1)
    valid = jrow <= cur
    forced = (jrow == 0) | (jrow == cur) | (jrow == cur - 1)
    score = jnp.where(forced, jnp.inf, jnp.where(valid, imp[:nsb, :], -jnp.inf))
    rank = jnp.zeros((nsb, tq), jnp.int32)
    for jp in range(nsb):
        other = score[jp:jp + 1, :]
        ahead = (other > score) | ((other == score) & (jrow > jp))
        rank = rank + jnp.where(ahead, 1, 0)
    chosen = jnp.where((rank < N_SEL) & valid, 0.0, MASK_VALUE)
    selb_ref[...] = per_head(chosen)

    width = hpg * tq

    def key_mask(start, size, lower):
        ktok = lax.broadcasted_iota(jnp.int32, (size, tq), 0) + start
        qtok = lax.broadcasted_iota(jnp.int32, (size, tq), 1) + t0
        ok = ktok <= qtok
        if lower:
            ok = ok & (ktok > qtok - WINDOW)
        return per_head(jnp.where(ok, 0.0, MASK_VALUE))

    per_chunk = chunk // SEL_BLOCK

    def sel_scores(c):
        start = pl.multiple_of(c * chunk, chunk)
        st = _dot_nt(ks_ref[pl.ds(start, chunk), :], qs)
        rows = [jnp.broadcast_to(selb_ref[pl.ds(c * per_chunk + r, 1), :], (SEL_BLOCK, width))
                for r in range(per_chunk)]
        return st + jnp.concatenate(rows, axis=0), start

    def accumulate(carry, st, start):
        m, l, acc = carry
        m_new = jnp.maximum(m, jnp.max(st, axis=0, keepdims=True))
        p = jnp.exp2((st - m_new) * EXP2_SCALE)
        alpha = jnp.exp2((m - m_new) * EXP2_SCALE)
        l = alpha * l + jnp.sum(p, axis=0, keepdims=True)
        acc = alpha * acc + _dot(vst_ref[:, pl.ds(start, chunk)], p.astype(BF16))
        return m_new, l, acc

    def sel_body(c, carry):
        st, start = sel_scores(c)
        return accumulate(carry, st, start)

    last = qi // (chunk // tq)
    carry = (jnp.full((1, width), MASK_VALUE, F32), jnp.zeros((1, width), F32),
             jnp.zeros((HEAD_DIM, width), F32))
    carry = lax.fori_loop(0, last, sel_body, carry)
    st, start = sel_scores(last)
    _, l, acc = accumulate(carry, st + key_mask(start, chunk, False), start)
    o_sel = acc * (1.0 / l)

    wsize = WINDOW + tq
    ws = pl.multiple_of(jnp.maximum(t0 - WINDOW, 0), tq)
    st = _dot_nt(kw_ref[pl.ds(ws, wsize), :], qs) + key_mask(ws, wsize, True)
    p = jnp.exp2((st - jnp.max(st, axis=0, keepdims=True)) * EXP2_SCALE)
    l = jnp.sum(p, axis=0, keepdims=True)
    o_win = _dot(vwt_ref[:, pl.ds(ws, wsize)], p.astype(BF16)) * (1.0 / l)

    gate = _sigmoid(glt_ref[...].astype(F32))
    for h in range(hpg):
        cols = slice(h * tq, (h + 1) * tq)
        o = (gate[3 * h:3 * h + 1, :] * ocmp_ref[:, cols]
             + gate[3 * h + 1:3 * h + 2, :] * o_sel[:, cols]
             + gate[3 * h + 2:3 * h + 3, :] * o_win[:, cols])
        o_ref[:, h * HEAD_DIM:(h + 1) * HEAD_DIM] = o.T.astype(o_ref.dtype)


_NSA_KC, _NSA_KS, _NSA_KW = 12, 14, 16
_NSA_VC, _NSA_VS = 18, 20
_NSA_QMEM_COL = 24 * HEAD_DIM
_NSA_GATE = 28
NSA_PROJ_COLS = 30 * HEAD_DIM
NSA_ROPE_COLS = 18 * HEAD_DIM


def _group_major_t(u, batch, seq, kinds):
    u = u.reshape(batch, seq, kinds, N_KV_GROUPS, HEAD_DIM)
    return u.transpose(2, 0, 3, 4, 1)


def nsa_attention(proj, cmp_kv, batch, seq, *, tq, chunk):
    nq = seq // tq
    n_chunks = seq // CMP_STRIDE
    n_cmp = n_chunks - 1
    ns = seq // SEL_BLOCK
    assert n_chunks == LANES and ns <= LANES and ns % 8 == 0 and WINDOW % tq == 0 and tq == LANES
    assert seq % chunk == 0 and chunk % tq == 0 and chunk % SEL_BLOCK == 0 and seq >= WINDOW + tq
    c0 = jnp.arange(LANES)[None, :] * CMP_STRIDE
    s0 = jnp.arange(LANES)[:, None] * SEL_BLOCK
    ovt = jnp.maximum(jnp.minimum(c0 + CMP_LEN, s0 + SEL_BLOCK) - jnp.maximum(c0, s0), 0)
    ovt = (ovt.astype(F32) / CMP_STRIDE).astype(BF16)
    ovt2 = jnp.concatenate([ovt, ovt], axis=1)
    vct = jnp.swapaxes(cmp_kv[1], -1, -2)
    v_t = _group_major_t(proj[:, _NSA_VS * HEAD_DIM:(_NSA_VS + 2 * N_KV_GROUPS) * HEAD_DIM], batch, seq, 2)
    gate_t = _group_major_t(proj[:, _NSA_GATE * HEAD_DIM:(_NSA_GATE + N_KV_GROUPS) * HEAD_DIM], batch, seq, 1)

    def k_spec(col0):
        return pl.BlockSpec((seq, HEAD_DIM), lambda b, g, i: (b, col0 + g))

    def vt_spec(which):
        return pl.BlockSpec((None, None, None, HEAD_DIM, seq), lambda b, g, i: (which, b, g, 0, 0))

    return pl.pallas_call(
        functools.partial(_nsa_kernel, tq=tq, n_cmp=n_cmp, n_sel_blocks=ns, chunk=chunk),
        grid=(batch, N_KV_GROUPS, nq),
        in_specs=[
            pl.BlockSpec((tq, HEADS_PER_GROUP * HEAD_DIM), lambda b, g, i: (b * nq + i, g)),
            pl.BlockSpec((None, None, None, n_chunks, HEAD_DIM), lambda b, g, i: (0, b, g, 0, 0)),
            pl.BlockSpec((None, None, HEAD_DIM, n_chunks), lambda b, g, i: (b, g, 0, 0)),
            k_spec(_NSA_KS), vt_spec(0), k_spec(_NSA_KW), vt_spec(1),
            pl.BlockSpec((None, None, None, LANES, tq), lambda b, g, i: (0, b, g, 0, i)),
            pl.BlockSpec((LANES, 2 * LANES), lambda b, g, i: (0, 0)),
        ],
        out_specs=pl.BlockSpec((tq, HEADS_PER_GROUP * HEAD_DIM), lambda b, g, i: (b * nq + i, g)),
        out_shape=jax.ShapeDtypeStruct((batch * seq, MIX_COLS), BF16),
        scratch_shapes=[pltpu.VMEM((HEAD_DIM, HEADS_PER_GROUP * tq), F32),
                        pltpu.VMEM((ns, HEADS_PER_GROUP * tq), F32)],
        compiler_params=_cparams(("parallel", "parallel", "arbitrary"), 48),
        name="nsa_attention",
    )(proj, cmp_kv, vct, proj, v_t, proj, v_t, gate_t, ovt2)


def _mem_kernel(q_ref, kv_ref, o_ref):
    for h in range(N_MEM_HEADS):
        cols = slice(h * HEAD_DIM, (h + 1) * HEAD_DIM)
        k = kv_ref[:, cols]
        v = kv_ref[:, MEM_COLS + h * HEAD_DIM:MEM_COLS + (h + 1) * HEAD_DIM]
        s = _dot_nt(q_ref[:, cols], k) * ATTN_SCALE
        e = jnp.exp(s - jnp.max(s, axis=-1, keepdims=True))
        p = e / jnp.sum(e, axis=-1, keepdims=True)
        o_ref[:, cols] = _dot(p.astype(BF16), v).astype(o_ref.dtype)


def mem_attention(proj, qmem_col, mem_kv, batch, seq, n_mem, *, tq):
    nq = seq // tq
    qblk = qmem_col // MEM_COLS
    assert qmem_col % MEM_COLS == 0
    return pl.pallas_call(
        _mem_kernel,
        grid=(batch, nq),
        in_specs=[
            pl.BlockSpec((tq, MEM_COLS), lambda b, i: (b * nq + i, qblk)),
            pl.BlockSpec((n_mem, 2 * MEM_COLS), lambda b, i: (b, 0)),
        ],
        out_specs=pl.BlockSpec((tq, MEM_COLS), lambda b, i: (b * nq + i, 0)),
        out_shape=jax.ShapeDtypeStruct((batch * seq, MEM_COLS), BF16),
        compiler_params=_cparams(("parallel", "arbitrary"), 32),
        name="mem_attention",
    )(proj, mem_kv)


def _matmul_norm_res_kernel(a_ref, w_ref, g_ref, x_ref, o_ref, acc_ref):
    k = pl.program_id(1)

    @pl.when(k == 0)
    def _():
        acc_ref[...] = jnp.zeros_like(acc_ref)

    acc_ref[...] += _dot(a_ref[...], w_ref[...])

    @pl.when(k == pl.num_programs(1) - 1)
    def _():
        o_ref[...] = x_ref[...] + _rms_rows(acc_ref[...], g_ref[...])


def matmul_norm_residual(a, w, g, x, *, tm, tk):
    m, kdim = a.shape
    d = w.shape[1]
    assert m % tm == 0 and kdim % tk == 0
    return pl.pallas_call(
        _matmul_norm_res_kernel,
        grid=(m // tm, kdim // tk),
        in_specs=[
            pl.BlockSpec((tm, tk), lambda i, k: (i, k)),
            pl.BlockSpec((tk, d), lambda i, k: (k, 0)),
            pl.BlockSpec((1, d), lambda i, k: (0, 0)),
            pl.BlockSpec((tm, d), lambda i, k: (i, 0)),
        ],
        out_specs=pl.BlockSpec((tm, d), lambda i, k: (i, 0)),
        out_shape=jax.ShapeDtypeStruct((m, d), F32),
        scratch_shapes=[pltpu.VMEM((tm, d), F32)],
        compiler_params=_cparams(("parallel", "arbitrary"), 48),
        name="matmul_norm_residual",
    )(a, w, g.reshape(1, d), x)


def _ffn_up_kernel(x_ref, halo_ref, g_ref, wg_ref, wu_ref, cw_ref, cb_ref, o_ref, hn_ref, *, seq, norm_chunk):
    i, j = pl.program_id(0), pl.program_id(1)
    tm = x_ref.shape[0]

    @pl.when(j == 0)
    def _():
        def body(c, _):
            r = pl.multiple_of(c * norm_chunk, norm_chunk)
            hn_ref[pl.ds(r, norm_chunk), :] = _rms_rows(x_ref[pl.ds(r, norm_chunk), :], g_ref[...]).astype(BF16)
            return 0
        lax.fori_loop(0, tm // norm_chunk, body, 0)
        keep = jnp.where((i * tm) % seq == 0, 0.0, 1.0)
        hn_ref[tm:, :] = (_rms_rows(halo_ref[...], g_ref[...]) * keep).astype(BF16)

    hn = hn_ref[...]
    gate = _dot(hn, wg_ref[...])
    up = _dot(hn[:tm, :], wu_ref[...])
    g1 = pltpu.roll(gate, 1, 0)[:tm, :]
    g2 = pltpu.roll(gate, 2, 0)[:tm, :]
    gc = cb_ref[...] + g2 * cw_ref[0:1, :] + g1 * cw_ref[1:2, :] + gate[:tm, :] * cw_ref[2:3, :]
    o_ref[...] = ((gc * _sigmoid(gc)) * up).astype(o_ref.dtype)


def ffn_up(x, g, w_gate, w_up, conv_w, conv_b, seq, *, tm, tf):
    m, d = x.shape
    f = w_gate.shape[1]
    assert m % tm == 0 and f % tf == 0 and seq % tm == 0 and tm % CONV_HALO == 0
    halo_blocks = tm // CONV_HALO
    kern = functools.partial(_ffn_up_kernel, seq=seq, norm_chunk=min(tm, 128))
    return pl.pallas_call(
        kern,
        grid=(m // tm, f // tf),
        in_specs=[
            pl.BlockSpec((tm, d), lambda i, j: (i, 0)),
            pl.BlockSpec((CONV_HALO, d), lambda i, j: (jnp.maximum(i * halo_blocks - 1, 0), 0)),
            pl.BlockSpec((1, d), lambda i, j: (0, 0)),
            pl.BlockSpec((d, tf), lambda i, j: (0, j)),
            pl.BlockSpec((d, tf), lambda i, j: (0, j)),
            pl.BlockSpec((CONV_WIDTH, tf), lambda i, j: (0, j)),
            pl.BlockSpec((1, tf), lambda i, j: (0, j)),
        ],
        out_specs=pl.BlockSpec((tm, tf), lambda i, j: (i, j)),
        out_shape=jax.ShapeDtypeStruct((m, f), BF16),
        scratch_shapes=[pltpu.VMEM((tm + CONV_HALO, d), BF16)],
        compiler_params=_cparams(("parallel", "arbitrary"), 48),
        name="ffn_up",
    )(x, x, g.reshape(1, d), w_gate, w_up, conv_w, conv_b.reshape(1, f))


def _repack_nsa_weight(w):
    q, kc, vc, ks, vs, kw, vw, gl, qm = jnp.split(
        w, [MIX_COLS + KV_COLS * n for n in range(7)] + [MIX_COLS + 6 * KV_COLS + N_BRANCH * N_MIX_HEADS], axis=1)
    per_group = N_BRANCH * HEADS_PER_GROUP
    pad = jnp.zeros((w.shape[0], LANES - per_group), w.dtype)
    gates = [jnp.concatenate([gl[:, g * per_group:(g + 1) * per_group], pad], axis=1) for g in range(N_KV_GROUPS)]
    return jnp.concatenate([q, kc, ks, kw, vc, vs, vw, qm] + gates, axis=1)


def _cmp_chunks(proj, col_block, batch, seq):
    u = proj[:, col_block * HEAD_DIM:(col_block + N_KV_GROUPS) * HEAD_DIM]
    u = u.reshape(batch, seq // CMP_STRIDE, CMP_STRIDE, N_KV_GROUPS, HEAD_DIM)
    return u.transpose(0, 3, 1, 2, 4).reshape(batch, N_KV_GROUPS, seq // CMP_STRIDE, CMP_STRIDE * HEAD_DIM)


def kernel(x, mem, positions, norm_g, w_in_sb, w_in_nsa, cmp_pe, cmp_w1, cmp_w2, w_mem_kv, w_o,
           w_ffn_gate, w_ffn_up, ffn_conv_w, ffn_conv_b, w_ffn_down):
    batch, seq, d = x.shape
    n_mem = mem.shape[1]
    depth = norm_g.shape[0]
    m = batch * seq
    cosf, sinf = rope_tables(positions)
    xs = x.reshape(m, d)
    mem2 = mem.reshape(batch * n_mem, d)
    tm = 512

    for i in range(depth):
        j = i // 2
        if i % 2 == 0:
            proj = norm_matmul(xs, norm_g[i, 0], w_in_sb[j].astype(BF16), cosf, sinf, tm=tm, tn=1024)
            o_mix = sb_attention(proj, batch, seq, tq=256, heads=4)
            qmem_col = 3 * MIX_COLS
        else:
            proj = norm_matmul(xs, norm_g[i, 0], _repack_nsa_weight(w_in_nsa[j]).astype(BF16), cosf, sinf,
                               tm=tm, tn=768, rope_cols=NSA_ROPE_COLS)
            chunks = jnp.stack([_cmp_chunks(proj, _NSA_KC, batch, seq), _cmp_chunks(proj, _NSA_VC, batch, seq)])
            cmp_kv = cmp_tokens(chunks, cmp_pe[j].reshape(2, 2, CMP_STRIDE * HEAD_DIM),
                                cmp_w1[j].astype(BF16), cmp_w2[j].astype(BF16))
            o_mix = nsa_attention(proj, cmp_kv, batch, seq, tq=128, chunk=512)
            qmem_col = _NSA_QMEM_COL
        mem_kv = norm_matmul(mem2, norm_g[i, 4], w_mem_kv[i].astype(BF16), cosf, sinf, tm=n_mem, tn=1024)
        o_mem = mem_attention(proj, qmem_col, mem_kv, batch, seq, n_mem, tq=512)
        attn_in = jnp.concatenate([o_mix, o_mem], axis=1)
        xs = matmul_norm_residual(attn_in, w_o[i].astype(BF16), norm_g[i, 1], xs, tm=tm, tk=1024)
        act = ffn_up(xs, norm_g[i, 2], w_ffn_gate[i].astype(BF16), w_ffn_up[i].astype(BF16),
                     ffn_conv_w[i], ffn_conv_b[i], seq, tm=tm, tf=512)
        xs = matmul_norm_residual(act, w_ffn_down[i].astype(BF16), norm_g[i, 3], xs, tm=tm, tk=512)
    return xs.reshape(batch, seq, d)
```

```python
import functools
import math

import jax
import jax.numpy as jnp
from jax import lax
from jax.experimental import pallas as pl
from jax.experimental.pallas import tpu as pltpu

F32 = jnp.float32
BF16 = jnp.bfloat16

HEAD_DIM = 128
N_MIX_HEADS = 12
N_MEM_HEADS = 4
N_KV_GROUPS = 2
HEADS_PER_GROUP = N_MIX_HEADS // N_KV_GROUPS
N_BRANCH = 3
CMP_LEN = 32
CMP_STRIDE = 16
SEL_BLOCK = 64
N_SEL = 16
WINDOW = 512
CONV_WIDTH = 3
ROPE_THETA = 10000.0
NORM_EPS = 1e-6
ATTN_SCALE = HEAD_DIM ** -0.5
EXP2_SCALE = ATTN_SCALE * math.log2(math.e)
MASK_VALUE = -1e30

LANES = 128
MIX_COLS = N_MIX_HEADS * HEAD_DIM
MEM_COLS = N_MEM_HEADS * HEAD_DIM
KV_COLS = N_KV_GROUPS * HEAD_DIM
CONV_HALO = 16

_NT = (((1,), (1,)), ((), ()))


def _cparams(semantics, vmem_mib):
    return pltpu.CompilerParams(dimension_semantics=semantics,
                                vmem_limit_bytes=vmem_mib * 1024 * 1024)


def _dot(a, b):
    return jnp.dot(a, b, preferred_element_type=F32)


def _dot_nt(a, b):
    return lax.dot_general(a, b, _NT, preferred_element_type=F32)


def _split_bf16(x, axis):
    hi = x.astype(BF16)
    lo = (x - hi.astype(F32)).astype(BF16)
    return jnp.concatenate([hi, lo], axis=axis)


def _sigmoid(x):
    return 1.0 / (1.0 + jnp.exp(-x))


def _rms_rows(x, g):
    ms = jnp.mean(x * x, axis=-1, keepdims=True)
    return (x * lax.rsqrt(ms + NORM_EPS)) * g


def _rope_kernel(pos_ref, inv_ref, sign_ref, cos_ref, sin_ref):
    ang = pos_ref[...].astype(F32) * inv_ref[...]
    cos_ref[...] = jnp.cos(ang)
    sin_ref[...] = jnp.sin(ang) * sign_ref[...]


def rope_tables(positions):
    m = positions.size
    tm = min(m, 2048)
    inv_freq = jnp.power(ROPE_THETA, -jnp.arange(0, HEAD_DIM, 2, dtype=F32) / HEAD_DIM)
    inv_full = jnp.concatenate([inv_freq, inv_freq])[None, :]
    sign = jnp.concatenate([-jnp.ones((HEAD_DIM // 2,), F32), jnp.ones((HEAD_DIM // 2,), F32)])[None, :]
    row = pl.BlockSpec((tm, LANES), lambda i: (i, 0))
    const = pl.BlockSpec((1, LANES), lambda i: (0, 0))
    return pl.pallas_call(
        _rope_kernel,
        grid=(m // tm,),
        in_specs=[pl.BlockSpec((tm, 1), lambda i: (i, 0)), const, const],
        out_specs=[row, row],
        out_shape=[jax.ShapeDtypeStruct((m, LANES), F32)] * 2,
        compiler_params=_cparams(("parallel",), 32),
        name="rope_tables",
    )(positions.reshape(m, 1), inv_full, sign)


def _norm_matmul_kernel(x_ref, g_ref, w_ref, cos_ref, sin_ref, o_ref, hn_ref, *, rope_tiles, norm_chunk):
    j = pl.program_id(1)
    tm = x_ref.shape[0]

    @pl.when(j == 0)
    def _():
        def body(c, _):
            r = pl.multiple_of(c * norm_chunk, norm_chunk)
            hn_ref[pl.ds(r, norm_chunk), :] = _rms_rows(x_ref[pl.ds(r, norm_chunk), :], g_ref[...]).astype(BF16)
            return 0
        lax.fori_loop(0, tm // norm_chunk, body, 0)

    y = _dot(hn_ref[...], w_ref[...])

    if rope_tiles == 0:
        o_ref[...] = y.astype(o_ref.dtype)
    else:
        @pl.when(j < rope_tiles)
        def _():
            cosf, sinf = cos_ref[...], sin_ref[...]
            for c in range(y.shape[1] // HEAD_DIM):
                u = y[:, c * HEAD_DIM:(c + 1) * HEAD_DIM]
                o_ref[:, c * HEAD_DIM:(c + 1) * HEAD_DIM] = (
                    u * cosf + pltpu.roll(u, HEAD_DIM // 2, 1) * sinf).astype(o_ref.dtype)

        @pl.when(j >= rope_tiles)
        def _():
            o_ref[...] = y.astype(o_ref.dtype)


def norm_matmul(x, g, w, cosf, sinf, *, tm, tn, rope_cols=0):
    m, d = x.shape
    n = w.shape[1]
    assert m % tm == 0 and n % tn == 0 and rope_cols % tn == 0
    kern = functools.partial(_norm_matmul_kernel, rope_tiles=rope_cols // tn, norm_chunk=min(tm, 128))
    return pl.pallas_call(
        kern,
        grid=(m // tm, n // tn),
        in_specs=[
            pl.BlockSpec((tm, d), lambda i, j: (i, 0)),
            pl.BlockSpec((1, d), lambda i, j: (0, 0)),
            pl.BlockSpec((d, tn), lambda i, j: (0, j)),
            pl.BlockSpec((tm, LANES), lambda i, j: (i, 0)),
            pl.BlockSpec((tm, LANES), lambda i, j: (i, 0)),
        ],
        out_specs=pl.BlockSpec((tm, tn), lambda i, j: (i, j)),
        out_shape=jax.ShapeDtypeStruct((m, n), BF16),
        scratch_shapes=[pltpu.VMEM((tm, d), BF16)],
        compiler_params=_cparams(("parallel", "arbitrary"), 56),
        name="norm_matmul",
    )(x, g.reshape(1, d), w, cosf, sinf)


def _sb_kernel(q_ref, k_ref, vt_ref, l_ref, o_ref, *, tq, heads):
    qi = pl.program_id(2)
    key = lax.broadcasted_iota(jnp.int32, (tq, tq), 0)
    qry = lax.broadcasted_iota(jnp.int32, (tq, tq), 1)
    causal = key < qry
    hs = range(heads)
    cols = [slice(h * HEAD_DIM, (h + 1) * HEAD_DIM) for h in hs]

    def scores(kb):
        start = pl.multiple_of(kb * tq, tq)
        return [_dot_nt(k_ref[pl.ds(start, tq), cols[h]], q_ref[:, cols[h]]) for h in hs]

    def blocks(raw, kb, cs, diagonal):
        start = pl.multiple_of(kb * tq, tq)
        z = [raw[h] * ATTN_SCALE for h in hs]
        e = [jnp.exp2(jnp.abs(raw[h]) * (-EXP2_SCALE)) for h in hs]
        log_beta = [jnp.minimum(z[h], 0.0) - jnp.log(1.0 + e[h]) for h in hs]
        log_keep = [log_beta[h] - z[h] for h in hs]
        if diagonal:
            log_keep = [jnp.where(causal, log_keep[h], 0.0) for h in hs]
        later = [_dot(l_ref[...], log_keep[h].astype(BF16)) for h in hs]
        a = [jnp.exp(log_beta[h] + later[h] + cs[h]) for h in hs]
        if diagonal:
            a = [jnp.where(causal, a[h], 0.0) for h in hs]
        pv = [_dot(vt_ref[h, :, pl.ds(start, tq)], a[h].astype(BF16)) for h in hs]
        return pv, [cs[h] + later[h][0:1, :] + log_keep[h][0:1, :] for h in hs]

    acc, cs = blocks(scores(qi), qi, [jnp.zeros((1, tq), F32)] * heads, True)

    def body(i, carry):
        acc, cs = carry
        pv, cs = blocks(scores(qi - i), qi - i, cs, False)
        return [acc[h] + pv[h] for h in hs], cs

    acc, _ = lax.fori_loop(1, qi + 1, body, (acc, cs))
    for h in hs:
        o_ref[:, cols[h]] = acc[h].T.astype(o_ref.dtype)


def sb_attention(proj, batch, seq, *, tq, heads):
    nq = seq // tq
    hb = N_MIX_HEADS // heads
    assert N_MIX_HEADS % heads == 0
    width = heads * HEAD_DIM
    v_t = proj[:, 2 * MIX_COLS:3 * MIX_COLS].reshape(batch, seq, N_MIX_HEADS, HEAD_DIM).transpose(0, 2, 3, 1)
    s = lax.broadcasted_iota(jnp.int32, (tq, tq), 0)
    j = lax.broadcasted_iota(jnp.int32, (tq, tq), 1)
    later = (j > s).astype(BF16)
    return pl.pallas_call(
        functools.partial(_sb_kernel, tq=tq, heads=heads),
        grid=(batch, hb, nq),
        in_specs=[
            pl.BlockSpec((tq, width), lambda b, hh, i: (b * nq + i, hh)),
            pl.BlockSpec((seq, width), lambda b, hh, i: (b, hb + hh)),
            pl.BlockSpec((None, heads, HEAD_DIM, seq), lambda b, hh, i: (b, hh, 0, 0)),
            pl.BlockSpec((tq, tq), lambda b, hh, i: (0, 0)),
        ],
        out_specs=pl.BlockSpec((tq, width), lambda b, hh, i: (b * nq + i, hh)),
        out_shape=jax.ShapeDtypeStruct((batch * seq, MIX_COLS), BF16),
        compiler_params=_cparams(("parallel", "parallel", "arbitrary"), 32),
        name="sb_attention",
    )(proj, proj, v_t, later)


def _cmp_kernel(x_ref, pe_ref, w1_ref, w2_ref, o_ref):
    half = CMP_STRIDE * HEAD_DIM
    x = x_ref[...].astype(F32)
    pe = pe_ref[...]
    first = _dot((x + pe[0:1, :]).astype(BF16), w1_ref[:half, :])
    second = _dot((x + pe[1:2, :]).astype(BF16), w1_ref[half:, :])
    n = x.shape[0]
    hid = first + pltpu.roll(second, n - 1, 0)
    hid = hid * _sigmoid(hid)
    o_ref[...] = _dot(hid.astype(BF16), w2_ref[...]).astype(o_ref.dtype)


def cmp_tokens(chunks, pe, w1, w2):
    _, b, g, n, half = chunks.shape
    hid = w1.shape[-1]
    return pl.pallas_call(
        _cmp_kernel,
        grid=(2, b, g),
        in_specs=[
            pl.BlockSpec((None, None, None, n, half), lambda kv, bb, gg: (kv, bb, gg, 0, 0)),
            pl.BlockSpec((None, 2, half), lambda kv, bb, gg: (kv, 0, 0)),
            pl.BlockSpec((None, 2 * half, hid), lambda kv, bb, gg: (kv, 0, 0)),
            pl.BlockSpec((None, hid, HEAD_DIM), lambda kv, bb, gg: (kv, 0, 0)),
        ],
        out_specs=pl.BlockSpec((None, None, None, n, HEAD_DIM), lambda kv, bb, gg: (kv, bb, gg, 0, 0)),
        out_shape=jax.ShapeDtypeStruct((2, b, g, n, HEAD_DIM), BF16),
        compiler_params=_cparams(("parallel", "parallel", "parallel"), 32),
        name="cmp_tokens",
    )(chunks, pe, w1, w2)


def _nsa_kernel(q_ref, kc_ref, vct_ref, ks_ref, vst_ref, kw_ref, vwt_ref, glt_ref, ovt_ref,
                o_ref, ocmp_ref, selb_ref, *, tq, n_cmp, n_sel_blocks, chunk):
    qi = pl.program_id(2)
    hpg = HEADS_PER_GROUP
    t0 = qi * tq
    tok = lax.broadcasted_iota(jnp.int32, (LANES, tq), 1) + t0
    sub = lax.broadcasted_iota(jnp.int32, (LANES, tq), 0)
    qs =jnp.concatenate([q_ref[:, h * HEAD_DIM:(h + 1) * HEAD_DIM] for h in range(hpg)], axis=0)

    def per_head(x):
        return jnp.concatenate([x] * hpg, axis=1)

    c_ok = (sub * CMP_STRIDE + (CMP_LEN - 1) <= tok) & (sub < n_cmp)
    s = _dot_nt(kc_ref[...], qs) * ATTN_SCALE + per_head(jnp.where(c_ok, 0.0, MASK_VALUE))
    e = jnp.exp(s - jnp.max(s, axis=0, keepdims=True))
    tok_row = lax.broadcasted_iota(jnp.int32, (1, tq), 1) + t0
    has_block = per_head(jnp.where(tok_row >= CMP_LEN - 1, 1.0, 0.0))
    p = e * (has_block / jnp.sum(e, axis=0, keepdims=True))
    ocmp_ref[...] = _dot(vct_ref[...], p.astype(BF16))
    p_sum = p[:, 0:tq]
    for h in range(1, hpg):
        p_sum = p_sum + p[:, h * tq:(h + 1) * tq]

    imp = _dot(ovt_ref[...], _split_bf16(p_sum, 0))
    nsb = n_sel_blocks
    jrow = lax.broadcasted_iota(jnp.int32, (nsb, tq), 0)
    jtok = lax.broadcasted_iota(jnp.int32, (nsb, tq), 1) + t0
    cur = jnp.right_shift(jtok, SEL_BLOCK.bit_length() - 1)
    valid = jrow <= cur
    forced = (jrow == 0) | (jrow == cur) | (jrow == cur - 1)
    score = jnp.where(forced, jnp.inf, jnp.where(valid, imp[:nsb, :], -jnp.inf))
    rank = jnp.zeros((nsb, tq), jnp.int32)
    for jp in range(nsb):
        other = score[jp:jp + 1, :]
        ahead = (other > score) | ((other == score) & (jrow > jp))
        rank = rank + jnp.where(ahead, 1, 0)
    chosen = jnp.where((rank < N_SEL) & valid, 0.0, MASK_VALUE)
    selb_ref[...] = per_head(chosen)

    width = hpg * tq

    def key_mask(start, size, lower):
        ktok = lax.broadcasted_iota(jnp.int32, (size, tq), 0) + start
        qtok = lax.broadcasted_iota(jnp.int32, (size, tq), 1) + t0
        ok = ktok <= qtok
        if lower:
            ok = ok & (ktok > qtok - WINDOW)
        return per_head(jnp.where(ok, 0.0, MASK_VALUE))

    per_chunk = chunk // SEL_BLOCK

    def sel_scores(c):
        start = pl.multiple_of(c * chunk, chunk)
        st = _dot_nt(ks_ref[pl.ds(start, chunk), :], qs)
        rows = [jnp.broadcast_to(selb_ref[pl.ds(c * per_chunk + r, 1), :], (SEL_BLOCK, width))
                for r in range(per_chunk)]
        return st + jnp.concatenate(rows, axis=0), start

    def accumulate(carry, st, start):
        m, l, acc = carry
        m_new = jnp.maximum(m, jnp.max(st, axis=0, keepdims=True))
        p = jnp.exp2((st - m_new) * EXP2_SCALE)
        alpha = jnp.exp2((m - m_new) * EXP2_SCALE)
        l = alpha * l + jnp.sum(p, axis=0, keepdims=True)
        acc = alpha * acc + _dot(vst_ref[:, pl.ds(start, chunk)], p.astype(BF16))
        return m_new, l, acc

    def sel_body(c, carry):
        st, start = sel_scores(c)
        return accumulate(carry, st, start)

    last = qi // (chunk // tq)
    carry = (jnp.full((1, width), MASK_VALUE, F32), jnp.zeros((1, width), F32),
             jnp.zeros((HEAD_DIM, width), F32))
    carry = lax.fori_loop(0, last, sel_body, carry)
    st, start = sel_scores(last)
    _, l, acc = accumulate(carry, st + key_mask(start, chunk, False), start)
    o_sel = acc * (1.0 / l)

    wsize = WINDOW + tq
    ws = pl.multiple_of(jnp.maximum(t0 - WINDOW, 0), tq)
    st = _dot_nt(kw_ref[pl.ds(ws, wsize), :], qs) + key_mask(ws, wsize, True)
    p = jnp.exp2((st - jnp.max(st, axis=0, keepdims=True)) * EXP2_SCALE)
    l = jnp.sum(p, axis=0, keepdims=True)
    o_win = _dot(vwt_ref[:, pl.ds(ws, wsize)], p.astype(BF16)) * (1.0 / l)

    gate = _sigmoid(glt_ref[...].astype(F32))
    for h in range(hpg):
        cols = slice(h * tq, (h + 1) * tq)
        o = (gate[3 * h:3 * h + 1, :] * ocmp_ref[:, cols]
             + gate[3 * h + 1:3 * h + 2, :] * o_sel[:, cols]
             + gate[3 * h + 2:3 * h + 3, :] * o_win[:, cols])
        o_ref[:, h * HEAD_DIM:(h + 1) * HEAD_DIM] = o.T.astype(o_ref.dtype)


_NSA_KC, _NSA_KS, _NSA_KW = 12, 14, 16
_NSA_VC, _NSA_VS = 18, 20
_NSA_QMEM_COL = 24 * HEAD_DIM
_NSA_GATE = 28
NSA_PROJ_COLS = 30 * HEAD_DIM
NSA_ROPE_COLS = 18 * HEAD_DIM


def _group_major_t(u, batch, seq, kinds):
    u = u.reshape(batch, seq, kinds, N_KV_GROUPS, HEAD_DIM)
    return u.transpose(2, 0, 3, 4, 1)


def nsa_attention(proj, cmp_kv, batch, seq, *, tq, chunk):
    nq = seq // tq
    n_chunks = seq // CMP_STRIDE
    n_cmp = n_chunks - 1
    ns = seq // SEL_BLOCK
    assert n_chunks == LANES and ns <= LANES and ns % 8 == 0 and WINDOW % tq == 0 and tq == LANES
    assert seq % chunk == 0 and chunk % tq == 0 and chunk % SEL_BLOCK == 0 and seq >= WINDOW + tq
    c0 = jnp.arange(LANES)[None, :] * CMP_STRIDE
    s0 = jnp.arange(LANES)[:, None] * SEL_BLOCK
    ovt = jnp.maximum(jnp.minimum(c0 + CMP_LEN, s0 + SEL_BLOCK) - jnp.maximum(c0, s0), 0)
    ovt = (ovt.astype(F32) / CMP_STRIDE).astype(BF16)
    ovt2 = jnp.concatenate([ovt, ovt], axis=1)
    vct = jnp.swapaxes(cmp_kv[1], -1, -2)
    v_t = _group_major_t(proj[:, _NSA_VS * HEAD_DIM:(_NSA_VS + 2 * N_KV_GROUPS) * HEAD_DIM], batch, seq, 2)
    gate_t = _group_major_t(proj[:, _NSA_GATE * HEAD_DIM:(_NSA_GATE + N_KV_GROUPS) * HEAD_DIM], batch, seq, 1)

    def k_spec(col0):
        return pl.BlockSpec((seq, HEAD_DIM), lambda b, g, i: (b, col0 + g))

    def vt_spec(which):
        return pl.BlockSpec((None, None, None, HEAD_DIM, seq), lambda b, g, i: (which, b, g, 0, 0))

    return pl.pallas_call(
        functools.partial(_nsa_kernel, tq=tq, n_cmp=n_cmp, n_sel_blocks=ns, chunk=chunk),
        grid=(batch, N_KV_GROUPS, nq),
        in_specs=[
            pl.BlockSpec((tq, HEADS_PER_GROUP * HEAD_DIM), lambda b, g, i: (b * nq + i, g)),
            pl.BlockSpec((None, None, None, n_chunks, HEAD_DIM), lambda b, g, i: (0, b, g, 0, 0)),
            pl.BlockSpec((None, None, HEAD_DIM, n_chunks), lambda b, g, i: (b, g, 0, 0)),
            k_spec(_NSA_KS), vt_spec(0), k_spec(_NSA_KW), vt_spec(1),
            pl.BlockSpec((None, None, None, LANES, tq), lambda b, g, i: (0, b, g, 0, i)),
            pl.BlockSpec((LANES, 2 * LANES), lambda b, g, i: (0, 0)),
        ],
        out_specs=pl.BlockSpec((tq, HEADS_PER_GROUP * HEAD_DIM), lambda b, g, i: (b * nq + i, g)),
        out_shape=jax.ShapeDtypeStruct((batch * seq, MIX_COLS), BF16),
        scratch_shapes=[pltpu.VMEM((HEAD_DIM, HEADS_PER_GROUP * tq), F32),
                        pltpu.VMEM((ns, HEADS_PER_GROUP * tq), F32)],
        compiler_params=_cparams(("parallel", "parallel", "arbitrary"), 48),
        name="nsa_attention",
    )(proj, cmp_kv, vct, proj, v_t, proj, v_t, gate_t, ovt2)


def _mem_kernel(q_ref, kv_ref, o_ref):
    for h in range(N_MEM_HEADS):
        cols = slice(h * HEAD_DIM, (h + 1) * HEAD_DIM)
        k = kv_ref[:, cols]
        v = kv_ref[:, MEM_COLS + h * HEAD_DIM:MEM_COLS + (h + 1) * HEAD_DIM]
        s = _dot_nt(q_ref[:, cols], k) * ATTN_SCALE
        e = jnp.exp(s - jnp.max(s, axis=-1, keepdims=True))
        p = e / jnp.sum(e, axis=-1, keepdims=True)
        o_ref[:, cols] = _dot(p.astype(BF16), v).astype(o_ref.dtype)


def mem_attention(proj, qmem_col, mem_kv, batch, seq, n_mem, *, tq):
    nq = seq // tq
    qblk = qmem_col // MEM_COLS
    assert qmem_col % MEM_COLS == 0
    return pl.pallas_call(
        _mem_kernel,
        grid=(batch, nq),
        in_specs=[
            pl.BlockSpec((tq, MEM_COLS), lambda b, i: (b * nq + i, qblk)),
            pl.BlockSpec((n_mem, 2 * MEM_COLS), lambda b, i: (b, 0)),
        ],
        out_specs=pl.BlockSpec((tq, MEM_COLS), lambda b, i: (b * nq + i, 0)),
        out_shape=jax.ShapeDtypeStruct((batch * seq, MEM_COLS), BF16),
        compiler_params=_cparams(("parallel", "arbitrary"), 32),
        name="mem_attention",
    )(proj, mem_kv)


def _matmul_norm_res_kernel(*refs, n_in, norm_chunk):
    a_refs, w_refs = refs[:n_in], refs[n_in:2 * n_in]
    g_ref, x_ref, o_ref, y_ref = refs[2 * n_in:]
    j = pl.program_id(1)
    tm, tn = x_ref.shape[0], w_refs[0].shape[1]

    y = _dot(a_refs[0][...], w_refs[0][...])
    for a_ref, w_ref in zip(a_refs[1:], w_refs[1:]):
        y = y + _dot(a_ref[...], w_ref[...])
    y_ref[:, pl.ds(pl.multiple_of(j * tn, tn), tn)] = y

    @pl.when(j == pl.num_programs(1) - 1)
    def _():
        def body(c, _):
            r = pl.multiple_of(c * norm_chunk, norm_chunk)
            rows = pl.ds(r, norm_chunk)
            o_ref[rows, :] = x_ref[rows, :] + _rms_rows(y_ref[rows, :], g_ref[...])
            return 0
        lax.fori_loop(0, tm // norm_chunk, body, 0)


def matmul_norm_residual(parts, w, g, x, *, tm, tn):
    m, d = x.shape
    assert m % tm == 0 and d % tn == 0 and sum(p.shape[1] for p in parts) == w.shape[0]
    a_specs, w_specs, row0 = [], [], 0
    for p in parts:
        kp = p.shape[1]
        assert row0 % kp == 0
        a_specs.append(pl.BlockSpec((tm, kp), lambda i, j: (i, 0)))
        w_specs.append(pl.BlockSpec((kp, tn), lambda i, j, rb=row0 // kp: (rb, j)))
        row0 += kp
    kern = functools.partial(_matmul_norm_res_kernel, n_in=len(parts), norm_chunk=min(tm, 128))
    return pl.pallas_call(
        kern,
        grid=(m // tm, d // tn),
        in_specs=a_specs + w_specs + [
            pl.BlockSpec((1, d), lambda i, j: (0, 0)),
            pl.BlockSpec((tm, d), lambda i, j: (i, 0)),
        ],
        out_specs=pl.BlockSpec((tm, d), lambda i, j: (i, 0)),
        out_shape=jax.ShapeDtypeStruct((m, d), F32),
        scratch_shapes=[pltpu.VMEM((tm, d), F32)],
        compiler_params=_cparams(("parallel", "arbitrary"), 56),
        name="matmul_norm_residual",
    )(*parts, *([w] * len(parts)), g.reshape(1, d), x)


def _ffn_up_kernel(x_ref, halo_ref, g_ref, wg_ref, wu_ref, cw_ref, cb_ref, o_ref, hn_ref, *, seq, norm_chunk):
    i, j = pl.program_id(0), pl.program_id(1)
    tm = x_ref.shape[0]

    @pl.when(j == 0)
    def _():
        def body(c, _):
            r = pl.multiple_of(c * norm_chunk, norm_chunk)
            hn_ref[pl.ds(r, norm_chunk), :] = _rms_rows(x_ref[pl.ds(r, norm_chunk), :], g_ref[...]).astype(BF16)
            return 0
        lax.fori_loop(0, tm // norm_chunk, body, 0)
        keep = jnp.where((i * tm) % seq == 0, 0.0, 1.0)
        hn_ref[tm:, :] = (_rms_rows(halo_ref[...], g_ref[...]) * keep).astype(BF16)

    hn = hn_ref[...]
    gate = _dot(hn, wg_ref[...])
    up = _dot(hn[:tm, :], wu_ref[...])
    g1 = pltpu.roll(gate, 1, 0)[:tm, :]
    g2 = pltpu.roll(gate, 2, 0)[:tm, :]
    gc = cb_ref[...] + g2 * cw_ref[0:1, :] + g1 * cw_ref[1:2, :] + gate[:tm, :] * cw_ref[2:3, :]
    o_ref[...] = ((gc * _sigmoid(gc)) * up).astype(o_ref.dtype)


def ffn_up(x, g, w_gate, w_up, conv_w, conv_b, seq, *, tm, tf):
    m, d = x.shape
    f = w_gate.shape[1]
    assert m % tm == 0 and f % tf == 0 and seq % tm == 0 and tm % CONV_HALO == 0
    halo_blocks = tm // CONV_HALO
    kern = functools.partial(_ffn_up_kernel, seq=seq, norm_chunk=min(tm, 128))
    return pl.pallas_call(
        kern,
        grid=(m // tm, f // tf),
        in_specs=[
            pl.BlockSpec((tm, d), lambda i, j: (i, 0)),
            pl.BlockSpec((CONV_HALO, d), lambda i, j: (jnp.maximum(i * halo_blocks - 1, 0), 0)),
            pl.BlockSpec((1, d), lambda i, j: (0, 0)),
            pl.BlockSpec((d, tf), lambda i, j: (0, j)),
            pl.BlockSpec((d, tf), lambda i, j: (0, j)),
            pl.BlockSpec((CONV_WIDTH, tf), lambda i, j: (0, j)),
            pl.BlockSpec((1, tf), lambda i, j: (0, j)),
        ],
        out_specs=pl.BlockSpec((tm, tf), lambda i, j: (i, j)),
        out_shape=jax.ShapeDtypeStruct((m, f), BF16),
        scratch_shapes=[pltpu.VMEM((tm + CONV_HALO, d), BF16)],
        compiler_params=_cparams(("parallel", "arbitrary"), 56),
        name="ffn_up",
    )(x, x, g.reshape(1, d), w_gate, w_up, conv_w, conv_b.reshape(1, f))


def _repack_nsa_weight(w):
    q, kc, vc, ks, vs, kw, vw, gl, qm = jnp.split(
        w, [MIX_COLS + KV_COLS * n for n in range(7)] + [MIX_COLS + 6 * KV_COLS + N_BRANCH * N_MIX_HEADS], axis=1)
    per_group = N_BRANCH * HEADS_PER_GROUP
    pad = jnp.zeros((w.shape[0], LANES - per_group), w.dtype)
    gates = [jnp.concatenate([gl[:, g * per_group:(g + 1) * per_group], pad], axis=1) for g in range(N_KV_GROUPS)]
    return jnp.concatenate([q, kc, ks, kw, vc, vs, vw, qm] + gates, axis=1)


def _cmp_chunks(proj, col_block, batch, seq):
    u = proj[:, col_block * HEAD_DIM:(col_block + N_KV_GROUPS) * HEAD_DIM]
    u = u.reshape(batch, seq // CMP_STRIDE, CMP_STRIDE, N_KV_GROUPS, HEAD_DIM)
    return u.transpose(0, 3, 1, 2, 4).reshape(batch, N_KV_GROUPS, seq // CMP_STRIDE, CMP_STRIDE * HEAD_DIM)


def kernel(x, mem, positions, norm_g, w_in_sb, w_in_nsa, cmp_pe, cmp_w1, cmp_w2, w_mem_kv, w_o,
           w_ffn_gate, w_ffn_up, ffn_conv_w, ffn_conv_b, w_ffn_down):
    batch, seq, d = x.shape
    n_mem = mem.shape[1]
    depth = norm_g.shape[0]
    m = batch * seq
    cosf, sinf = rope_tables(positions)
    xs = x.reshape(m, d)
    mem2 = mem.reshape(batch * n_mem, d)
    tm = 1024

    for i in range(depth):
        j = i // 2
        if i % 2 == 0:
            proj = norm_matmul(xs, norm_g[i, 0], w_in_sb[j].astype(BF16), cosf, sinf, tm=tm, tn=1024)
            o_mix = sb_attention(proj, batch, seq, tq=256, heads=6)
            qmem_col = 3 * MIX_COLS
        else:
            proj = norm_matmul(xs, norm_g[i, 0], _repack_nsa_weight(w_in_nsa[j]).astype(BF16), cosf, sinf,
                               tm=tm, tn=768, rope_cols=NSA_ROPE_COLS)
            chunks = jnp.stack([_cmp_chunks(proj, _NSA_KC, batch, seq), _cmp_chunks(proj, _NSA_VC, batch, seq)])
            cmp_kv = cmp_tokens(chunks, cmp_pe[j].reshape(2, 2, CMP_STRIDE * HEAD_DIM),
                                cmp_w1[j].astype(BF16), cmp_w2[j].astype(BF16))
            o_mix = nsa_attention(proj, cmp_kv, batch, seq, tq=128, chunk=512)
            qmem_col = _NSA_QMEM_COL
        mem_kv = norm_matmul(mem2, norm_g[i, 4], w_mem_kv[i].astype(BF16), cosf, sinf, tm=n_mem, tn=1024)
        o_mem = mem_attention(proj, qmem_col, mem_kv, batch, seq, n_mem, tq=512)
        xs = matmul_norm_residual([o_mix, o_mem], w_o[i].astype(BF16), norm_g[i, 1], xs, tm=512, tn=d)
        act = ffn_up(xs, norm_g[i, 2], w_ffn_gate[i].astype(BF16), w_ffn_up[i].astype(BF16),
                     ffn_conv_w[i], ffn_conv_b[i], seq, tm=tm, tf=512)
        xs = matmul_norm_residual([act], w_ffn_down[i].astype(BF16), norm_g[i, 3], xs, tm=512, tn=512)
    return xs.reshape(batch, seq, d)
```

```python
import functools
import math

import jax
import jax.numpy as jnp
from jax import lax
from jax.experimental import pallas as pl
from jax.experimental.pallas import tpu as pltpu

F32 = jnp.float32
BF16 = jnp.bfloat16

HEAD_DIM = 128
N_MIX_HEADS = 12
N_MEM_HEADS = 4
N_KV_GROUPS = 2
HEADS_PER_GROUP = N_MIX_HEADS // N_KV_GROUPS
N_BRANCH = 3
CMP_LEN = 32
CMP_STRIDE = 16
SEL_BLOCK = 64
N_SEL = 16
WINDOW = 512
CONV_WIDTH = 3
ROPE_THETA = 10000.0
NORM_EPS = 1e-6
ATTN_SCALE = HEAD_DIM ** -0.5
EXP2_SCALE = ATTN_SCALE * math.log2(math.e)
MASK_VALUE = -1e30

LANES = 128
MIX_COLS = N_MIX_HEADS * HEAD_DIM
MEM_COLS = N_MEM_HEADS * HEAD_DIM
KV_COLS = N_KV_GROUPS * HEAD_DIM
CONV_HALO = 16

_NT = (((1,), (1,)), ((), ()))


def _cparams(semantics, vmem_mib):
    return pltpu.CompilerParams(dimension_semantics=semantics,
                                vmem_limit_bytes=vmem_mib * 1024 * 1024)


def _dot(a, b):
    return jnp.dot(a, b, preferred_element_type=F32)


def _dot_nt(a, b):
    return lax.dot_general(a, b, _NT, preferred_element_type=F32)


def _split_bf16(x, axis):
    hi = x.astype(BF16)
    lo = (x - hi.astype(F32)).astype(BF16)
    return jnp.concatenate([hi, lo], axis=axis)


def _sigmoid(x):
    return 1.0 / (1.0 + jnp.exp(-x))


def _rms_rows(x, g):
    ms = jnp.mean(x * x, axis=-1, keepdims=True)
    return (x * lax.rsqrt(ms + NORM_EPS)) * g


def _rope_kernel(pos_ref, inv_ref, sign_ref, cos_ref, sin_ref):
    ang = pos_ref[...].astype(F32) * inv_ref[...]
    cos_ref[...] = jnp.cos(ang)
    sin_ref[...] = jnp.sin(ang) * sign_ref[...]


def rope_tables(positions):
    m = positions.size
    tm = min(m, 2048)
    inv_freq = jnp.power(ROPE_THETA, -jnp.arange(0, HEAD_DIM, 2, dtype=F32) / HEAD_DIM)
    inv_full = jnp.concatenate([inv_freq, inv_freq])[None, :]
    sign = jnp.concatenate([-jnp.ones((HEAD_DIM // 2,), F32), jnp.ones((HEAD_DIM // 2,), F32)])[None, :]
    row = pl.BlockSpec((tm, LANES), lambda i: (i, 0))
    const = pl.BlockSpec((1, LANES), lambda i: (0, 0))
    return pl.pallas_call(
        _rope_kernel,
        grid=(m // tm,),
        in_specs=[pl.BlockSpec((tm, 1), lambda i: (i, 0)), const, const],
        out_specs=[row, row],
        out_shape=[jax.ShapeDtypeStruct((m, LANES), F32)] * 2,
        compiler_params=_cparams(("parallel",), 32),
        name="rope_tables",
    )(positions.reshape(m, 1), inv_full, sign)


def _norm_matmul_kernel(x_ref, g_ref, w_ref, cos_ref, sin_ref, o_ref, *rest, rope_tiles, t_tiles, norm_chunk):
    hn_ref = rest[-1]
    j = pl.program_id(1)
    tm = x_ref.shape[0]

    @pl.when(j == 0)
    def _():
        def body(c, _):
            r = pl.multiple_of(c * norm_chunk, norm_chunk)
            hn_ref[pl.ds(r, norm_chunk), :] = _rms_rows(x_ref[pl.ds(r, norm_chunk), :], g_ref[...]).astype(BF16)
            return 0
        lax.fori_loop(0, tm // norm_chunk, body, 0)

    y = _dot(hn_ref[...], w_ref[...])

    if rope_tiles == 0:
        o_ref[...] = y.astype(o_ref.dtype)
    else:
        @pl.when(j < rope_tiles)
        def _():
            cosf, sinf = cos_ref[...], sin_ref[...]
            for c in range(y.shape[1] // HEAD_DIM):
                u = y[:, c * HEAD_DIM:(c + 1) * HEAD_DIM]
                o_ref[:, c * HEAD_DIM:(c + 1) * HEAD_DIM] = (
                    u * cosf + pltpu.roll(u, HEAD_DIM // 2, 1) * sinf).astype(o_ref.dtype)

        @pl.when(j >= rope_tiles)
        def _():
            o_ref[...] = y.astype(o_ref.dtype)

    if t_tiles is not None:
        first, count = t_tiles

        @pl.when((j >= first) & (j < first + count))
        def _():
            rest[0][...] = y.T.astype(rest[0].dtype)


def norm_matmul(x, g, w, cosf, sinf, *, tm, tn, rope_cols=0, t_tiles=None):
    m, d = x.shape
    n = w.shape[1]
    assert m % tm == 0 and n % tn == 0 and rope_cols % tn == 0
    kern = functools.partial(_norm_matmul_kernel, rope_tiles=rope_cols // tn, t_tiles=t_tiles,
                             norm_chunk=min(tm, 128))
    out_specs = [pl.BlockSpec((tm, tn), lambda i, j: (i, j))]
    out_shape = [jax.ShapeDtypeStruct((m, n), BF16)]
    if t_tiles is not None:
        first, count = t_tiles
        out_specs.append(pl.BlockSpec((tn, tm), lambda i, j: (jnp.clip(j - first, 0, count - 1), i)))
        out_shape.append(jax.ShapeDtypeStruct((count * tn, m), BF16))
    outs = pl.pallas_call(
        kern,
        grid=(m // tm, n // tn),
        in_specs=[
            pl.BlockSpec((tm, d), lambda i, j: (i, 0)),
            pl.BlockSpec((1, d), lambda i, j: (0, 0)),
            pl.BlockSpec((d, tn), lambda i, j: (0, j)),
            pl.BlockSpec((tm, LANES), lambda i, j: (i, 0)),
            pl.BlockSpec((tm, LANES), lambda i, j: (i, 0)),
        ],
        out_specs=out_specs,
        out_shape=out_shape,
        scratch_shapes=[pltpu.VMEM((tm, d), BF16)],
        compiler_params=_cparams(("parallel", "arbitrary"), 56),
        name="norm_matmul",
    )(x, g.reshape(1, d), w, cosf, sinf)
    return outs if t_tiles is not None else outs[0]


def _sb_kernel(q_ref, k_ref, vt_ref, l_ref, o_ref, *, tq, heads):
    qi = pl.program_id(2)
    key = lax.broadcasted_iota(jnp.int32, (tq, tq), 0)
    qry = lax.broadcasted_iota(jnp.int32, (tq, tq), 1)
    causal = key < qry
    hs = range(heads)
    cols = [slice(h * HEAD_DIM, (h + 1) * HEAD_DIM) for h in hs]

    def scores(kb):
        start = pl.multiple_of(kb * tq, tq)
        return [_dot_nt(k_ref[pl.ds(start, tq), cols[h]], q_ref[:, cols[h]]) for h in hs]

    def blocks(raw, kb, cs, diagonal):
        start = pl.multiple_of(kb * tq, tq)
        z = [raw[h] * ATTN_SCALE for h in hs]
        e = [jnp.exp2(jnp.abs(raw[h]) * (-EXP2_SCALE)) for h in hs]
        log_beta = [jnp.minimum(z[h], 0.0) - jnp.log(1.0 + e[h]) for h in hs]
        log_keep = [log_beta[h] - z[h] for h in hs]
        if diagonal:
            log_keep = [jnp.where(causal, log_keep[h], 0.0) for h in hs]
        later = [_dot(l_ref[...], log_keep[h].astype(BF16)) for h in hs]
        a = [jnp.exp(log_beta[h] + later[h] + cs[h]) for h in hs]
        if diagonal:
            a = [jnp.where(causal, a[h], 0.0) for h in hs]
        pv = [_dot(vt_ref[cols[h], pl.ds(start, tq)], a[h].astype(BF16)) for h in hs]
        return pv, [cs[h] + later[h][0:1, :] + log_keep[h][0:1, :] for h in hs]

    acc, cs = blocks(scores(qi), qi, [jnp.zeros((1, tq), F32)] * heads, True)

    def body(i, carry):
        acc, cs = carry
        pv, cs = blocks(scores(qi - i), qi - i, cs, False)
        return [acc[h] + pv[h] for h in hs], cs

    acc, _ = lax.fori_loop(1, qi + 1, body, (acc, cs))
    for h in hs:
        o_ref[:, cols[h]] = acc[h].T.astype(o_ref.dtype)


def sb_attention(proj, v_t, batch, seq, *, tq, heads):
    nq = seq // tq
    hb = N_MIX_HEADS // heads
    assert N_MIX_HEADS % heads == 0
    width = heads * HEAD_DIM
    s = lax.broadcasted_iota(jnp.int32, (tq, tq), 0)
    j = lax.broadcasted_iota(jnp.int32, (tq, tq), 1)
    later = (j > s).astype(BF16)
    return pl.pallas_call(
        functools.partial(_sb_kernel, tq=tq, heads=heads),
        grid=(batch, hb, nq),
        in_specs=[
            pl.BlockSpec((tq, width), lambda b, hh, i: (b * nq + i, hh)),
            pl.BlockSpec((seq, width), lambda b, hh, i: (b, hb + hh)),
            pl.BlockSpec((width, seq), lambda b, hh, i: (hh, b)),
            pl.BlockSpec((tq, tq), lambda b, hh, i: (0, 0)),
        ],
        out_specs=pl.BlockSpec((tq, width), lambda b, hh, i: (b * nq + i, hh)),
        out_shape=jax.ShapeDtypeStruct((batch * seq, MIX_COLS), BF16),
        compiler_params=_cparams(("parallel", "parallel", "arbitrary"), 32),
        name="sb_attention",
    )(proj, proj, v_t, later)


def _cmp_kernel(x_ref, pe_ref, w1_ref, w2_ref, o_ref):
    half = CMP_STRIDE * HEAD_DIM
    x = x_ref[...].astype(F32)
    pe = pe_ref[...]
    first = _dot((x + pe[0:1, :]).astype(BF16), w1_ref[:half, :])
    second = _dot((x + pe[1:2, :]).astype(BF16), w1_ref[half:, :])
    n = x.shape[0]
    hid = first + pltpu.roll(second, n - 1, 0)
    hid = hid * _sigmoid(hid)
    o_ref[...] = _dot(hid.astype(BF16), w2_ref[...]).astype(o_ref.dtype)


def cmp_tokens(chunks, pe, w1, w2):
    _, b, g, n, half = chunks.shape
    hid = w1.shape[-1]
    return pl.pallas_call(
        _cmp_kernel,
        grid=(2, b, g),
        in_specs=[
            pl.BlockSpec((None, None, None, n, half), lambda kv, bb, gg: (kv, bb, gg, 0, 0)),
            pl.BlockSpec((None, 2, half), lambda kv, bb, gg: (kv, 0, 0)),
            pl.BlockSpec((None, 2 * half, hid), lambda kv, bb, gg: (kv, 0, 0)),
            pl.BlockSpec((None, hid, HEAD_DIM), lambda kv, bb, gg: (kv, 0, 0)),
        ],
        out_specs=pl.BlockSpec((None, None, None, n, HEAD_DIM), lambda kv, bb, gg: (kv, bb, gg, 0, 0)),
        out_shape=jax.ShapeDtypeStruct((2, b, g, n, HEAD_DIM), BF16),
        compiler_params=_cparams(("parallel", "parallel", "parallel"), 32),
        name="cmp_tokens",
    )(chunks, pe, w1, w2)


def _nsa_kernel(q_ref, kc_ref, vct_ref, ks_ref, vst_ref, kw_ref, vwt_ref, glt_ref, ovt_ref,
                o_ref, ocmp_ref, selb_ref, *, tq, n_cmp, n_sel_blocks, chunk):
    qi = pl.program_id(2)
    hpg = HEADS_PER_GROUP
    t0 = qi * tq
    tok = lax.broadcasted_iota(jnp.int32, (LANES, tq), 1) + t0
    sub = lax.broadcasted_iota(jnp.int32, (LANES, tq), 0)
    qs =jnp.concatenate([q_ref[:, h * HEAD_DIM:(h + 1) * HEAD_DIM] for h in range(hpg)], axis=0)

    def per_head(x):
        return jnp.concatenate([x] * hpg, axis=1)

    c_ok = (sub * CMP_STRIDE + (CMP_LEN - 1) <= tok) & (sub < n_cmp)
    s = _dot_nt(kc_ref[...], qs) * ATTN_SCALE + per_head(jnp.where(c_ok, 0.0, MASK_VALUE))
    e = jnp.exp(s - jnp.max(s, axis=0, keepdims=True))
    tok_row = lax.broadcasted_iota(jnp.int32, (1, tq), 1) + t0
    has_block = per_head(jnp.where(tok_row >= CMP_LEN - 1, 1.0, 0.0))
    p = e * (has_block / jnp.sum(e, axis=0, keepdims=True))
    ocmp_ref[...] = _dot(vct_ref[...], p.astype(BF16))
    p_sum = p[:, 0:tq]
    for h in range(1, hpg):
        p_sum = p_sum + p[:, h * tq:(h + 1) * tq]

    imp = _dot(ovt_ref[...], _split_bf16(p_sum, 0))
    nsb = n_sel_blocks
    jrow = lax.broadcasted_iota(jnp.int32, (nsb, tq), 0)
    jtok = lax.broadcasted_iota(jnp.int32, (nsb, tq), 1) + t0
    cur = jnp.right_shift(jtok, SEL_BLOCK.bit_length() - 1)
    valid = jrow <= cur
    forced = (jrow == 0) | (jrow == cur) | (jrow == cur - 1)
    score = jnp.where(forced, jnp.inf, jnp.where(valid, imp[:nsb, :], -jnp.inf))
    rank = jnp.zeros((nsb, tq), jnp.int32)
    for jp in range(nsb):
        other = score[jp:jp + 1, :]
        ahead = (other > score) | ((other == score) & (jrow > jp))
        rank = rank + jnp.where(ahead, 1, 0)
    chosen = jnp.where((rank < N_SEL) & valid, 0.0, MASK_VALUE)
    selb_ref[...] = per_head(chosen)

    width = hpg * tq

    def key_mask(start, size, lower):
        ktok = lax.broadcasted_iota(jnp.int32, (size, tq), 0) + start
        qtok = lax.broadcasted_iota(jnp.int32, (size, tq), 1) + t0
        ok = ktok <= qtok
        if lower:
            ok = ok & (ktok > qtok - WINDOW)
        return per_head(jnp.where(ok, 0.0, MASK_VALUE))

    per_chunk = chunk // SEL_BLOCK

    def sel_scores(c):
        start = pl.multiple_of(c * chunk, chunk)
        st = _dot_nt(ks_ref[pl.ds(start, chunk), :], qs)
        rows = [jnp.broadcast_to(selb_ref[pl.ds(c * per_chunk + r, 1), :], (SEL_BLOCK, width))
                for r in range(per_chunk)]
        return st + jnp.concatenate(rows, axis=0), start

    def accumulate(carry, st, start):
        m, l, acc = carry
        m_new = jnp.maximum(m, jnp.max(st, axis=0, keepdims=True))
        p = jnp.exp2((st - m_new) * EXP2_SCALE)
        alpha = jnp.exp2((m - m_new) * EXP2_SCALE)
        l = alpha * l + jnp.sum(p, axis=0, keepdims=True)
        acc = alpha * acc + _dot(vst_ref[:, pl.ds(start, chunk)], p.astype(BF16))
        return m_new, l, acc

    def sel_body(c, carry):
        st, start = sel_scores(c)
        return accumulate(carry, st, start)

    last = qi // (chunk // tq)
    carry = (jnp.full((1, width), MASK_VALUE, F32), jnp.zeros((1, width), F32),
             jnp.zeros((HEAD_DIM, width), F32))
    carry = lax.fori_loop(0, last, sel_body, carry)
    st, start = sel_scores(last)
    _, l, acc = accumulate(carry, st + key_mask(start, chunk, False), start)
    o_sel = acc * (1.0 / l)

    wsize = WINDOW + tq
    ws = pl.multiple_of(jnp.maximum(t0 - WINDOW, 0), tq)
    st = _dot_nt(kw_ref[pl.ds(ws, wsize), :], qs) + key_mask(ws, wsize, True)
    p = jnp.exp2((st - jnp.max(st, axis=0, keepdims=True)) * EXP2_SCALE)
    l = jnp.sum(p, axis=0, keepdims=True)
    o_win = _dot(vwt_ref[:, pl.ds(ws, wsize)], p.astype(BF16)) * (1.0 / l)

    gate = _sigmoid(glt_ref[...].astype(F32))
    for h in range(hpg):
        cols = slice(h * tq, (h + 1) * tq)
        o = (gate[3 * h:3 * h + 1, :] * ocmp_ref[:, cols]
             + gate[3 * h + 1:3 * h + 2, :] * o_sel[:, cols]
             + gate[3 * h + 2:3 * h + 3, :] * o_win[:, cols])
        o_ref[:, h * HEAD_DIM:(h + 1) * HEAD_DIM] = o.T.astype(o_ref.dtype)


_NSA_KC, _NSA_KS, _NSA_KW = 12, 14, 16
_NSA_VC, _NSA_VS = 18, 20
_NSA_QMEM_COL = 24 * HEAD_DIM
_NSA_GATE = 28
NSA_PROJ_COLS = 30 * HEAD_DIM
NSA_ROPE_COLS = 18 * HEAD_DIM


_NSA_T_FIRST_COL = _NSA_VC
_NSA_T_VS = _NSA_VS - _NSA_T_FIRST_COL
_NSA_T_VW = _NSA_T_VS + N_KV_GROUPS
_NSA_T_GATE = _NSA_GATE - _NSA_T_FIRST_COL


def nsa_attention(proj, proj_t, cmp_kv, batch, seq, *, tq, chunk):
    nq = seq // tq
    n_chunks = seq // CMP_STRIDE
    n_cmp = n_chunks - 1
    ns = seq // SEL_BLOCK
    assert n_chunks == LANES and ns <= LANES and ns % 8 == 0 and WINDOW % tq == 0 and tq == LANES
    assert seq % chunk == 0 and chunk % tq == 0 and chunk % SEL_BLOCK == 0 and seq >= WINDOW + tq
    c0 = jnp.arange(LANES)[None, :] * CMP_STRIDE
    s0 = jnp.arange(LANES)[:, None] * SEL_BLOCK
    ovt = jnp.maximum(jnp.minimum(c0 + CMP_LEN, s0 + SEL_BLOCK) - jnp.maximum(c0, s0), 0)
    ovt = (ovt.astype(F32) / CMP_STRIDE).astype(BF16)
    ovt2 = jnp.concatenate([ovt, ovt], axis=1)
    vct = jnp.swapaxes(cmp_kv[1], -1, -2)

    def k_spec(col0):
        return pl.BlockSpec((seq, HEAD_DIM), lambda b, g, i: (b, col0 + g))

    def vt_spec(row0):
        return pl.BlockSpec((HEAD_DIM, seq), lambda b, g, i: (row0 + g, b))

    return pl.pallas_call(
        functools.partial(_nsa_kernel, tq=tq, n_cmp=n_cmp, n_sel_blocks=ns, chunk=chunk),
        grid=(batch, N_KV_GROUPS, nq),
        in_specs=[
            pl.BlockSpec((tq, HEADS_PER_GROUP * HEAD_DIM), lambda b, g, i: (b * nq + i, g)),
            pl.BlockSpec((None, None, None, n_chunks, HEAD_DIM), lambda b, g, i: (0, b, g, 0, 0)),
            pl.BlockSpec((None, None, HEAD_DIM, n_chunks), lambda b, g, i: (b, g, 0, 0)),
            k_spec(_NSA_KS), vt_spec(_NSA_T_VS), k_spec(_NSA_KW), vt_spec(_NSA_T_VW),
            pl.BlockSpec((LANES, tq), lambda b, g, i: (_NSA_T_GATE + g, b * nq + i)),
            pl.BlockSpec((LANES, 2 * LANES), lambda b, g, i: (0, 0)),
        ],
        out_specs=pl.BlockSpec((tq, HEADS_PER_GROUP * HEAD_DIM), lambda b, g, i: (b * nq + i, g)),
        out_shape=jax.ShapeDtypeStruct((batch * seq, MIX_COLS), BF16),
        scratch_shapes=[pltpu.VMEM((HEAD_DIM, HEADS_PER_GROUP * tq), F32),
                        pltpu.VMEM((ns, HEADS_PER_GROUP * tq), F32)],
        compiler_params=_cparams(("parallel", "parallel", "arbitrary"), 48),
        name="nsa_attention",
    )(proj, cmp_kv, vct, proj, proj_t, proj, proj_t, proj_t, ovt2)


def _mem_kernel(q_ref, kv_ref, o_ref):
    for h in range(N_MEM_HEADS):
        cols = slice(h * HEAD_DIM, (h + 1) * HEAD_DIM)
        k = kv_ref[:, cols]
        v = kv_ref[:, MEM_COLS + h * HEAD_DIM:MEM_COLS + (h + 1) * HEAD_DIM]
        s = _dot_nt(q_ref[:, cols], k) * ATTN_SCALE
        e = jnp.exp(s - jnp.max(s, axis=-1, keepdims=True))
        p = e / jnp.sum(e, axis=-1, keepdims=True)
        o_ref[:, cols] = _dot(p.astype(BF16), v).astype(o_ref.dtype)


def mem_attention(proj, qmem_col, mem_kv, batch, seq, n_mem, *, tq):
    nq = seq // tq
    qblk = qmem_col // MEM_COLS
    assert qmem_col % MEM_COLS == 0
    return pl.pallas_call(
        _mem_kernel,
        grid=(batch, nq),
        in_specs=[
            pl.BlockSpec((tq, MEM_COLS), lambda b, i: (b * nq + i, qblk)),
            pl.BlockSpec((n_mem, 2 * MEM_COLS), lambda b, i: (b, 0)),
        ],
        out_specs=pl.BlockSpec((tq, MEM_COLS), lambda b, i: (b * nq + i, 0)),
        out_shape=jax.ShapeDtypeStruct((batch * seq, MEM_COLS), BF16),
        compiler_params=_cparams(("parallel", "arbitrary"), 32),
        name="mem_attention",
    )(proj, mem_kv)


def _matmul_norm_res_kernel(*refs, n_in, norm_chunk):
    a_refs, w_refs = refs[:n_in], refs[n_in:2 * n_in]
    g_ref, x_ref, o_ref, y_ref = refs[2 * n_in:]
    j = pl.program_id(1)
    tm, tn = x_ref.shape[0], w_refs[0].shape[1]

    y = _dot(a_refs[0][...], w_refs[0][...])
    for a_ref, w_ref in zip(a_refs[1:], w_refs[1:]):
        y = y + _dot(a_ref[...], w_ref[...])
    y_ref[:, pl.ds(pl.multiple_of(j * tn, tn), tn)] = y

    @pl.when(j == pl.num_programs(1) - 1)
    def _():
        def body(c, _):
            r = pl.multiple_of(c * norm_chunk, norm_chunk)
            rows = pl.ds(r, norm_chunk)
            o_ref[rows, :] = x_ref[rows, :] + _rms_rows(y_ref[rows, :], g_ref[...])
            return 0
        lax.fori_loop(0, tm // norm_chunk, body, 0)


def matmul_norm_residual(parts, w, g, x, *, tm, tn):
    m, d = x.shape
    assert m % tm == 0 and d % tn == 0 and sum(p.shape[1] for p in parts) == w.shape[0]
    a_specs, w_specs, row0 = [], [], 0
    for p in parts:
        kp = p.shape[1]
        assert row0 % kp == 0
        a_specs.append(pl.BlockSpec((tm, kp), lambda i, j: (i, 0)))
        w_specs.append(pl.BlockSpec((kp, tn), lambda i, j, rb=row0 // kp: (rb, j)))
        row0 += kp
    kern = functools.partial(_matmul_norm_res_kernel, n_in=len(parts), norm_chunk=min(tm, 128))
    return pl.pallas_call(
        kern,
        grid=(m // tm, d // tn),
        in_specs=a_specs + w_specs + [
            pl.BlockSpec((1, d), lambda i, j: (0, 0)),
            pl.BlockSpec((tm, d), lambda i, j: (i, 0)),
        ],
        out_specs=pl.BlockSpec((tm, d), lambda i, j: (i, 0)),
        out_shape=jax.ShapeDtypeStruct((m, d), F32),
        scratch_shapes=[pltpu.VMEM((tm, d), F32)],
        compiler_params=_cparams(("parallel", "arbitrary"), 56),
        name="matmul_norm_residual",
    )(*parts, *([w] * len(parts)), g.reshape(1, d), x)


def _ffn_up_kernel(x_ref, halo_ref, g_ref, wg_ref, wu_ref, cw_ref, cb_ref, o_ref, hn_ref, *, seq, norm_chunk):
    i, j = pl.program_id(0), pl.program_id(1)
    tm = x_ref.shape[0]

    @pl.when(j == 0)
    def _():
        def body(c, _):
            r = pl.multiple_of(c * norm_chunk, norm_chunk)
            hn_ref[pl.ds(r, norm_chunk), :] = _rms_rows(x_ref[pl.ds(r, norm_chunk), :], g_ref[...]).astype(BF16)
            return 0
        lax.fori_loop(0, tm // norm_chunk, body, 0)
        keep = jnp.where((i * tm) % seq == 0, 0.0, 1.0)
        hn_ref[tm:, :] = (_rms_rows(halo_ref[...], g_ref[...]) * keep).astype(BF16)

    hn = hn_ref[...]
    gate = _dot(hn, wg_ref[...])
    up = _dot(hn[:tm, :], wu_ref[...])
    g1 = pltpu.roll(gate, 1, 0)[:tm, :]
    g2 = pltpu.roll(gate, 2, 0)[:tm, :]
    gc = cb_ref[...] + g2 * cw_ref[0:1, :] + g1 * cw_ref[1:2, :] + gate[:tm, :] * cw_ref[2:3, :]
    o_ref[...] = ((gc * _sigmoid(gc)) * up).astype(o_ref.dtype)


def ffn_up(x, g, w_gate, w_up, conv_w, conv_b, seq, *, tm, tf):
    m, d = x.shape
    f = w_gate.shape[1]
    assert m % tm == 0 and f % tf == 0 and seq % tm == 0 and tm % CONV_HALO == 0
    halo_blocks = tm // CONV_HALO
    kern = functools.partial(_ffn_up_kernel, seq=seq, norm_chunk=min(tm, 128))
    return pl.pallas_call(
        kern,
        grid=(m // tm, f // tf),
        in_specs=[
            pl.BlockSpec((tm, d), lambda i, j: (i, 0)),
            pl.BlockSpec((CONV_HALO, d), lambda i, j: (jnp.maximum(i * halo_blocks - 1, 0), 0)),
            pl.BlockSpec((1, d), lambda i, j: (0, 0)),
            pl.BlockSpec((d, tf), lambda i, j: (0, j)),
            pl.BlockSpec((d, tf), lambda i, j: (0, j)),
            pl.BlockSpec((CONV_WIDTH, tf), lambda i, j: (0, j)),
            pl.BlockSpec((1, tf), lambda i, j: (0, j)),
        ],
        out_specs=pl.BlockSpec((tm, tf), lambda i, j: (i, j)),
        out_shape=jax.ShapeDtypeStruct((m, f), BF16),
        scratch_shapes=[pltpu.VMEM((tm + CONV_HALO, d), BF16)],
        compiler_params=_cparams(("parallel", "arbitrary"), 56),
        name="ffn_up",
    )(x, x, g.reshape(1, d), w_gate, w_up, conv_w, conv_b.reshape(1, f))


def _repack_nsa_weight(w):
    q, kc, vc, ks, vs, kw, vw, gl, qm = jnp.split(
        w, [MIX_COLS + KV_COLS * n for n in range(7)] + [MIX_COLS + 6 * KV_COLS + N_BRANCH * N_MIX_HEADS], axis=1)
    per_group = N_BRANCH * HEADS_PER_GROUP
    pad = jnp.zeros((w.shape[0], LANES - per_group), w.dtype)
    gates = [jnp.concatenate([gl[:, g * per_group:(g + 1) * per_group], pad], axis=1) for g in range(N_KV_GROUPS)]
    return jnp.concatenate([q, kc, ks, kw, vc, vs, vw, qm] + gates, axis=1)


def _cmp_chunks(proj, col_block, batch, seq):
    u = proj[:, col_block * HEAD_DIM:(col_block + N_KV_GROUPS) * HEAD_DIM]
    u = u.reshape(batch, seq // CMP_STRIDE, CMP_STRIDE, N_KV_GROUPS, HEAD_DIM)
    return u.transpose(0, 3, 1, 2, 4).reshape(batch, N_KV_GROUPS, seq // CMP_STRIDE, CMP_STRIDE * HEAD_DIM)


def kernel(x, mem, positions, norm_g, w_in_sb, w_in_nsa, cmp_pe, cmp_w1, cmp_w2, w_mem_kv, w_o,
           w_ffn_gate, w_ffn_up, ffn_conv_w, ffn_conv_b, w_ffn_down):
    batch, seq, d = x.shape
    n_mem = mem.shape[1]
    depth = norm_g.shape[0]
    m = batch * seq
    cosf, sinf = rope_tables(positions)
    xs = x.reshape(m, d)
    mem2 = mem.reshape(batch * n_mem, d)
    tm = 1024

    for i in range(depth):
        j = i // 2
        if i % 2 == 0:
            tn = 512
            proj, v_t = norm_matmul(xs, norm_g[i, 0], w_in_sb[j].astype(BF16), cosf, sinf, tm=tm, tn=tn,
                                    t_tiles=(2 * MIX_COLS // tn, MIX_COLS // tn))
            o_mix = sb_attention(proj, v_t, batch, seq, tq=256, heads=6)
            qmem_col = 3 * MIX_COLS
        else:
            tn = 768
            proj, proj_t = norm_matmul(xs, norm_g[i, 0], _repack_nsa_weight(w_in_nsa[j]).astype(BF16), cosf, sinf,
                                       tm=tm, tn=tn, rope_cols=NSA_ROPE_COLS,
                                       t_tiles=(_NSA_T_FIRST_COL * HEAD_DIM // tn, 2))
            chunks = jnp.stack([_cmp_chunks(proj, _NSA_KC, batch, seq), _cmp_chunks(proj, _NSA_VC, batch, seq)])
            cmp_kv = cmp_tokens(chunks, cmp_pe[j].reshape(2, 2, CMP_STRIDE * HEAD_DIM),
                                cmp_w1[j].astype(BF16), cmp_w2[j].astype(BF16))
            o_mix = nsa_attention(proj, proj_t, cmp_kv, batch, seq, tq=128, chunk=512)
            qmem_col = _NSA_QMEM_COL
        mem_kv = norm_matmul(mem2, norm_g[i, 4], w_mem_kv[i].astype(BF16), cosf, sinf, tm=n_mem, tn=1024)
        o_mem = mem_attention(proj, qmem_col, mem_kv, batch, seq, n_mem, tq=512)
        xs = matmul_norm_residual([o_mix, o_mem], w_o[i].astype(BF16), norm_g[i, 1], xs, tm=512, tn=d)
        act = ffn_up(xs, norm_g[i, 2], w_ffn_gate[i].astype(BF16), w_ffn_up[i].astype(BF16),
                     ffn_conv_w[i], ffn_conv_b[i], seq, tm=tm, tf=512)
        xs = matmul_norm_residual([act], w_ffn_down[i].astype(BF16), norm_g[i, 3], xs, tm=512, tn=512)
    return xs.reshape(batch, seq, d)
```

```python
import functools
import math

import jax
import jax.numpy as jnp
from jax import lax
from jax.experimental import pallas as pl
from jax.experimental.pallas import tpu as pltpu

F32 = jnp.float32
BF16 = jnp.bfloat16

HEAD_DIM = 128
N_MIX_HEADS = 12
N_MEM_HEADS = 4
N_KV_GROUPS = 2
HEADS_PER_GROUP = N_MIX_HEADS // N_KV_GROUPS
N_BRANCH = 3
CMP_LEN = 32
CMP_STRIDE = 16
SEL_BLOCK = 64
N_SEL = 16
WINDOW = 512
CONV_WIDTH = 3
ROPE_THETA = 10000.0
NORM_EPS = 1e-6
ATTN_SCALE = HEAD_DIM ** -0.5
EXP2_SCALE = ATTN_SCALE * math.log2(math.e)
MASK_VALUE = -1e30

LANES = 128
MIX_COLS = N_MIX_HEADS * HEAD_DIM
MEM_COLS = N_MEM_HEADS * HEAD_DIM
KV_COLS = N_KV_GROUPS * HEAD_DIM
CONV_HALO = 16

_NT = (((1,), (1,)), ((), ()))


def _cparams(semantics, vmem_mib):
    return pltpu.CompilerParams(dimension_semantics=semantics,
                                vmem_limit_bytes=vmem_mib * 1024 * 1024)


def _dot(a, b):
    return jnp.dot(a, b, preferred_element_type=F32)


def _dot_nt(a, b):
    return lax.dot_general(a, b, _NT, preferred_element_type=F32)


def _split_bf16(x, axis):
    hi = x.astype(BF16)
    lo = (x - hi.astype(F32)).astype(BF16)
    return jnp.concatenate([hi, lo], axis=axis)


def _sigmoid(x):
    return 1.0 / (1.0 + jnp.exp(-x))


def _rms_rows(x, g):
    ms = jnp.mean(x * x, axis=-1, keepdims=True)
    return (x * lax.rsqrt(ms + NORM_EPS)) * g


def _rope_kernel(pos_ref, inv_ref, sign_ref, cos_ref, sin_ref):
    ang = pos_ref[...].astype(F32) * inv_ref[...]
    cos_ref[...] = jnp.cos(ang)
    sin_ref[...] = jnp.sin(ang) * sign_ref[...]


def rope_tables(positions):
    m = positions.size
    tm = min(m, 2048)
    inv_freq = jnp.power(ROPE_THETA, -jnp.arange(0, HEAD_DIM, 2, dtype=F32) / HEAD_DIM)
    inv_full = jnp.concatenate([inv_freq, inv_freq])[None, :]
    sign = jnp.concatenate([-jnp.ones((HEAD_DIM // 2,), F32), jnp.ones((HEAD_DIM // 2,), F32)])[None, :]
    row = pl.BlockSpec((tm, LANES), lambda i: (i, 0))
    const = pl.BlockSpec((1, LANES), lambda i: (0, 0))
    return pl.pallas_call(
        _rope_kernel,
        grid=(m // tm,),
        in_specs=[pl.BlockSpec((tm, 1), lambda i: (i, 0)), const, const],
        out_specs=[row, row],
        out_shape=[jax.ShapeDtypeStruct((m, LANES), F32)] * 2,
        compiler_params=_cparams(("parallel",), 32),
        name="rope_tables",
    )(positions.reshape(m, 1), inv_full, sign)


def _norm_matmul_kernel(x_ref, g_ref, w_ref, cos_ref, sin_ref, o_ref, *rest, rope_tiles, t_tiles, norm_chunk):
    hn_ref = rest[-1]
    j = pl.program_id(1)
    tm = x_ref.shape[0]

    @pl.when(j == 0)
    def _():
        def body(c, _):
            r = pl.multiple_of(c * norm_chunk, norm_chunk)
            hn_ref[pl.ds(r, norm_chunk), :] = _rms_rows(x_ref[pl.ds(r, norm_chunk), :], g_ref[...]).astype(BF16)
            return 0
        lax.fori_loop(0, tm // norm_chunk, body, 0)

    y = _dot(hn_ref[...], w_ref[...])

    if rope_tiles == 0:
        o_ref[...] = y.astype(o_ref.dtype)
    else:
        @pl.when(j < rope_tiles)
        def _():
            cosf, sinf = cos_ref[...], sin_ref[...]
            for c in range(y.shape[1] // HEAD_DIM):
                u = y[:, c * HEAD_DIM:(c + 1) * HEAD_DIM]
                o_ref[:, c * HEAD_DIM:(c + 1) * HEAD_DIM] = (
                    u * cosf + pltpu.roll(u, HEAD_DIM // 2, 1) * sinf).astype(o_ref.dtype)

        @pl.when(j >= rope_tiles)
        def _():
            o_ref[...] = y.astype(o_ref.dtype)

    if t_tiles is not None:
        first, count = t_tiles

        @pl.when((j >= first) & (j < first + count))
        def _():
            rest[0][...] = y.T.astype(rest[0].dtype)


def norm_matmul(x, g, w, layer, cosf, sinf, *, tm, tn, rope_cols=0, t_tiles=None):
    m, d = x.shape
    n = w.shape[2]
    assert m % tm == 0 and n % tn == 0 and rope_cols % tn == 0
    kern = functools.partial(_norm_matmul_kernel, rope_tiles=rope_cols // tn, t_tiles=t_tiles,
                             norm_chunk=min(tm, 128))
    out_specs = [pl.BlockSpec((tm, tn), lambda i, j: (i, j))]
    out_shape = [jax.ShapeDtypeStruct((m, n), BF16)]
    if t_tiles is not None:
        first, count = t_tiles
        out_specs.append(pl.BlockSpec((tn, tm), lambda i, j: (jnp.clip(j - first, 0, count - 1), i)))
        out_shape.append(jax.ShapeDtypeStruct((count * tn, m), BF16))
    outs = pl.pallas_call(
        kern,
        grid=(m // tm, n // tn),
        in_specs=[
            pl.BlockSpec((tm, d), lambda i, j: (i, 0)),
            pl.BlockSpec((1, d), lambda i, j: (0, 0)),
            pl.BlockSpec((None, d, tn), lambda i, j: (layer, 0, j)),
            pl.BlockSpec((tm, LANES), lambda i, j: (i, 0)),
            pl.BlockSpec((tm, LANES), lambda i, j: (i, 0)),
        ],
        out_specs=out_specs,
        out_shape=out_shape,
        scratch_shapes=[pltpu.VMEM((tm, d), BF16)],
        compiler_params=_cparams(("parallel", "arbitrary"), 56),
        name="norm_matmul",
    )(x, g.reshape(1, d), w, cosf, sinf)
    return outs if t_tiles is not None else outs[0]


def _sb_kernel(q_ref, k_ref, vt_ref, l_ref, o_ref, *, tq, heads):
    qi = pl.program_id(2)
    key = lax.broadcasted_iota(jnp.int32, (tq, tq), 0)
    qry = lax.broadcasted_iota(jnp.int32, (tq, tq), 1)
    causal = key < qry
    hs = range(heads)
    cols = [slice(h * HEAD_DIM, (h + 1) * HEAD_DIM) for h in hs]

    def scores(kb):
        start = pl.multiple_of(kb * tq, tq)
        return [_dot_nt(k_ref[pl.ds(start, tq), cols[h]], q_ref[:, cols[h]]) for h in hs]

    def blocks(raw, kb, cs, diagonal):
        start = pl.multiple_of(kb * tq, tq)
        z = [raw[h] * ATTN_SCALE for h in hs]
        e = [jnp.exp2(jnp.abs(raw[h]) * (-EXP2_SCALE)) for h in hs]
        log_beta = [jnp.minimum(z[h], 0.0) - jnp.log(1.0 + e[h]) for h in hs]
        log_keep = [log_beta[h] - z[h] for h in hs]
        if diagonal:
            log_keep = [jnp.where(causal, log_keep[h], 0.0) for h in hs]
        later = [_dot(l_ref[...], log_keep[h].astype(BF16)) for h in hs]
        a = [jnp.exp(log_beta[h] + later[h] + cs[h]) for h in hs]
        if diagonal:
            a = [jnp.where(causal, a[h], 0.0) for h in hs]
        pv = [_dot(vt_ref[cols[h], pl.ds(start, tq)], a[h].astype(BF16)) for h in hs]
        return pv, [cs[h] + later[h][0:1, :] + log_keep[h][0:1, :] for h in hs]

    acc, cs = blocks(scores(qi), qi, [jnp.zeros((1, tq), F32)] * heads, True)

    def body(i, carry):
        acc, cs = carry
        pv, cs = blocks(scores(qi - i), qi - i, cs, False)
        return [acc[h] + pv[h] for h in hs], cs

    acc, _ = lax.fori_loop(1, qi + 1, body, (acc, cs))
    for h in hs:
        o_ref[:, cols[h]] = acc[h].T.astype(o_ref.dtype)


def sb_attention(proj, v_t, batch, seq, *, tq, heads):
    nq = seq // tq
    hb = N_MIX_HEADS // heads
    assert N_MIX_HEADS % heads == 0
    width = heads * HEAD_DIM
    s = lax.broadcasted_iota(jnp.int32, (tq, tq), 0)
    j = lax.broadcasted_iota(jnp.int32, (tq, tq), 1)
    later = (j > s).astype(BF16)
    return pl.pallas_call(
        functools.partial(_sb_kernel, tq=tq, heads=heads),
        grid=(batch, hb, nq),
        in_specs=[
            pl.BlockSpec((tq, width), lambda b, hh, i: (b * nq + i, hh)),
            pl.BlockSpec((seq, width), lambda b, hh, i: (b, hb + hh)),
            pl.BlockSpec((width, seq), lambda b, hh, i: (hh, b)),
            pl.BlockSpec((tq, tq), lambda b, hh, i: (0, 0)),
        ],
        out_specs=pl.BlockSpec((tq, width), lambda b, hh, i: (b * nq + i, hh)),
        out_shape=jax.ShapeDtypeStruct((batch * seq, MIX_COLS), BF16),
        compiler_params=_cparams(("parallel", "parallel", "arbitrary"), 32),
        name="sb_attention",
    )(proj, proj, v_t, later)


def _cmp_kernel(x_ref, pe_ref, w1_ref, w2_ref, o_ref):
    half = CMP_STRIDE * HEAD_DIM
    x = x_ref[...].astype(F32)
    pe = pe_ref[...]
    first = _dot((x + pe[0:1, :]).astype(BF16), w1_ref[:half, :])
    second = _dot((x + pe[1:2, :]).astype(BF16), w1_ref[half:, :])
    n = x.shape[0]
    hid = first + pltpu.roll(second, n - 1, 0)
    hid = hid * _sigmoid(hid)
    o_ref[...] = _dot(hid.astype(BF16), w2_ref[...]).astype(o_ref.dtype)


def cmp_tokens(chunks, pe, w1, w2):
    _, g, b, n, half = chunks.shape
    hid = w1.shape[-1]
    return pl.pallas_call(
        _cmp_kernel,
        grid=(2, b, g),
        in_specs=[
            pl.BlockSpec((None, None, None, n, half), lambda kv, bb, gg: (kv, gg, bb, 0, 0)),
            pl.BlockSpec((None, 2, half), lambda kv, bb, gg: (kv, 0, 0)),
            pl.BlockSpec((None, 2 * half, hid), lambda kv, bb, gg: (kv, 0, 0)),
            pl.BlockSpec((None, hid, HEAD_DIM), lambda kv, bb, gg: (kv, 0, 0)),
        ],
        out_specs=pl.BlockSpec((None, None, None, n, HEAD_DIM), lambda kv, bb, gg: (kv, bb, gg, 0, 0)),
        out_shape=jax.ShapeDtypeStruct((2, b, g, n, HEAD_DIM), BF16),
        compiler_params=_cparams(("parallel", "parallel", "parallel"), 32),
        name="cmp_tokens",
    )(chunks, pe, w1, w2)


def _nsa_kernel(q_ref, kc_ref, vct_ref, ks_ref, vst_ref, kw_ref, vwt_ref, glt_ref, ovt_ref,
                o_ref, ocmp_ref, selb_ref, *, tq, n_cmp, n_sel_blocks, chunk):
    qi = pl.program_id(2)
    hpg = HEADS_PER_GROUP
    t0 = qi * tq
    tok = lax.broadcasted_iota(jnp.int32, (LANES, tq), 1) + t0
    sub = lax.broadcasted_iota(jnp.int32, (LANES, tq), 0)
    qs =jnp.concatenate([q_ref[:, h * HEAD_DIM:(h + 1) * HEAD_DIM] for h in range(hpg)], axis=0)

    def per_head(x):
        return jnp.concatenate([x] * hpg, axis=1)

    c_ok = (sub * CMP_STRIDE + (CMP_LEN - 1) <= tok) & (sub < n_cmp)
    s = _dot_nt(kc_ref[...], qs) * ATTN_SCALE + per_head(jnp.where(c_ok, 0.0, MASK_VALUE))
    e = jnp.exp(s - jnp.max(s, axis=0, keepdims=True))
    tok_row = lax.broadcasted_iota(jnp.int32, (1, tq), 1) + t0
    has_block = per_head(jnp.where(tok_row >= CMP_LEN - 1, 1.0, 0.0))
    p = e * (has_block / jnp.sum(e, axis=0, keepdims=True))
    ocmp_ref[...] = _dot(vct_ref[...], p.astype(BF16))
    p_sum = p[:, 0:tq]
    for h in range(1, hpg):
        p_sum = p_sum + p[:, h * tq:(h + 1) * tq]

    imp = _dot(ovt_ref[...], _split_bf16(p_sum, 0))
    nsb = n_sel_blocks
    jrow = lax.broadcasted_iota(jnp.int32, (nsb, tq), 0)
    jtok = lax.broadcasted_iota(jnp.int32, (nsb, tq), 1) + t0
    cur = jnp.right_shift(jtok, SEL_BLOCK.bit_length() - 1)
    valid = jrow <= cur
    forced = (jrow == 0) | (jrow == cur) | (jrow == cur - 1)
    score = jnp.where(forced, jnp.inf, jnp.where(valid, imp[:nsb, :], -jnp.inf))
    rank = jnp.zeros((nsb, tq), jnp.int32)
    for jp in range(nsb):
        other = score[jp:jp + 1, :]
        ahead = (other > score) | ((other == score) & (jrow > jp))
        rank = rank + jnp.where(ahead, 1, 0)
    chosen = jnp.where((rank < N_SEL) & valid, 0.0, MASK_VALUE)
    selb_ref[...] = per_head(chosen)

    width = hpg * tq

    def key_mask(start, size, lower):
        ktok = lax.broadcasted_iota(jnp.int32, (size, tq), 0) + start
        qtok = lax.broadcasted_iota(jnp.int32, (size, tq), 1) + t0
        ok = ktok <= qtok
        if lower:
            ok = ok & (ktok > qtok - WINDOW)
        return per_head(jnp.where(ok, 0.0, MASK_VALUE))

    per_chunk = chunk // SEL_BLOCK

    def sel_scores(c):
        start = pl.multiple_of(c * chunk, chunk)
        st = _dot_nt(ks_ref[pl.ds(start, chunk), :], qs)
        rows = [jnp.broadcast_to(selb_ref[pl.ds(c * per_chunk + r, 1), :], (SEL_BLOCK, width))
                for r in range(per_chunk)]
        return st + jnp.concatenate(rows, axis=0), start

    def accumulate(carry, st, start):
        m, l, acc = carry
        m_new = jnp.maximum(m, jnp.max(st, axis=0, keepdims=True))
        p = jnp.exp2((st - m_new) * EXP2_SCALE)
        alpha = jnp.exp2((m - m_new) * EXP2_SCALE)
        l = alpha * l + jnp.sum(p, axis=0, keepdims=True)
        acc = alpha * acc + _dot(vst_ref[:, pl.ds(start, chunk)], p.astype(BF16))
        return m_new, l, acc

    def sel_body(c, carry):
        st, start = sel_scores(c)
        return accumulate(carry, st, start)

    last = qi // (chunk // tq)
    carry = (jnp.full((1, width), MASK_VALUE, F32), jnp.zeros((1, width), F32),
             jnp.zeros((HEAD_DIM, width), F32))
    carry = lax.fori_loop(0, last, sel_body, carry)
    st, start = sel_scores(last)
    _, l, acc = accumulate(carry, st + key_mask(start, chunk, False), start)
    o_sel = acc * (1.0 / l)

    wsize = WINDOW + tq
    ws = pl.multiple_of(jnp.maximum(t0 - WINDOW, 0), tq)
    st = _dot_nt(kw_ref[pl.ds(ws, wsize), :], qs) + key_mask(ws, wsize, True)
    p = jnp.exp2((st - jnp.max(st, axis=0, keepdims=True)) * EXP2_SCALE)
    l = jnp.sum(p, axis=0, keepdims=True)
    o_win = _dot(vwt_ref[:, pl.ds(ws, wsize)], p.astype(BF16)) * (1.0 / l)

    gate = _sigmoid(glt_ref[...].astype(F32))
    for h in range(hpg):
        cols = slice(h * tq, (h + 1) * tq)
        o = (gate[3 * h:3 * h + 1, :] * ocmp_ref[:, cols]
             + gate[3 * h + 1:3 * h + 2, :] * o_sel[:, cols]
             + gate[3 * h + 2:3 * h + 3, :] * o_win[:, cols])
        o_ref[:, h * HEAD_DIM:(h + 1) * HEAD_DIM] = o.T.astype(o_ref.dtype)


_NSA_KC, _NSA_KS, _NSA_KW = 12, 14, 16
_NSA_VC, _NSA_VS = 18, 20
_NSA_QMEM_COL = 24 * HEAD_DIM
_NSA_GATE = 28
NSA_PROJ_COLS = 30 * HEAD_DIM
NSA_ROPE_COLS = 18 * HEAD_DIM


_NSA_T_FIRST_COL = _NSA_VC
_NSA_T_VS = _NSA_VS - _NSA_T_FIRST_COL
_NSA_T_VW = _NSA_T_VS + N_KV_GROUPS
_NSA_T_GATE = _NSA_GATE - _NSA_T_FIRST_COL


def nsa_attention(proj, proj_t, cmp_kv, batch, seq, *, tq, chunk):
    nq = seq // tq
    n_chunks = seq // CMP_STRIDE
    n_cmp = n_chunks - 1
    ns = seq // SEL_BLOCK
    assert n_chunks == LANES and ns <= LANES and ns % 8 == 0 and WINDOW % tq == 0 and tq == LANES
    assert seq % chunk == 0 and chunk % tq == 0 and chunk % SEL_BLOCK == 0 and seq >= WINDOW + tq
    c0 = jnp.arange(LANES)[None, :] * CMP_STRIDE
    s0 = jnp.arange(LANES)[:, None] * SEL_BLOCK
    ovt = jnp.maximum(jnp.minimum(c0 + CMP_LEN, s0 + SEL_BLOCK) - jnp.maximum(c0, s0), 0)
    ovt = (ovt.astype(F32) / CMP_STRIDE).astype(BF16)
    ovt2 = jnp.concatenate([ovt, ovt], axis=1)
    vct = jnp.swapaxes(cmp_kv[1], -1, -2)

    def k_spec(col0):
        return pl.BlockSpec((seq, HEAD_DIM), lambda b, g, i: (b, col0 + g))

    def vt_spec(row0):
        return pl.BlockSpec((HEAD_DIM, seq), lambda b, g, i: (row0 + g, b))

    return pl.pallas_call(
        functools.partial(_nsa_kernel, tq=tq, n_cmp=n_cmp, n_sel_blocks=ns, chunk=chunk),
        grid=(batch, N_KV_GROUPS, nq),
        in_specs=[
            pl.BlockSpec((tq, HEADS_PER_GROUP * HEAD_DIM), lambda b, g, i: (b * nq + i, g)),
            pl.BlockSpec((None, None, None, n_chunks, HEAD_DIM), lambda b, g, i: (0, b, g, 0, 0)),
            pl.BlockSpec((None, None, HEAD_DIM, n_chunks), lambda b, g, i: (b, g, 0, 0)),
            k_spec(_NSA_KS), vt_spec(_NSA_T_VS), k_spec(_NSA_KW), vt_spec(_NSA_T_VW),
            pl.BlockSpec((LANES, tq), lambda b, g, i: (_NSA_T_GATE + g, b * nq + i)),
            pl.BlockSpec((LANES, 2 * LANES), lambda b, g, i: (0, 0)),
        ],
        out_specs=pl.BlockSpec((tq, HEADS_PER_GROUP * HEAD_DIM), lambda b, g, i: (b * nq + i, g)),
        out_shape=jax.ShapeDtypeStruct((batch * seq, MIX_COLS), BF16),
        scratch_shapes=[pltpu.VMEM((HEAD_DIM, HEADS_PER_GROUP * tq), F32),
                        pltpu.VMEM((ns, HEADS_PER_GROUP * tq), F32)],
        compiler_params=_cparams(("parallel", "parallel", "arbitrary"), 48),
        name="nsa_attention",
    )(proj, cmp_kv, vct, proj, proj_t, proj, proj_t, proj_t, ovt2)


def _mem_kernel(q_ref, kv_ref, o_ref):
    for h in range(N_MEM_HEADS):
        cols = slice(h * HEAD_DIM, (h + 1) * HEAD_DIM)
        k = kv_ref[:, cols]
        v = kv_ref[:, MEM_COLS + h * HEAD_DIM:MEM_COLS + (h + 1) * HEAD_DIM]
        s = _dot_nt(q_ref[:, cols], k) * ATTN_SCALE
        e = jnp.exp(s - jnp.max(s, axis=-1, keepdims=True))
        p = e / jnp.sum(e, axis=-1, keepdims=True)
        o_ref[:, cols] = _dot(p.astype(BF16), v).astype(o_ref.dtype)


def mem_attention(proj, qmem_col, mem_kv, batch, seq, n_mem, *, tq):
    nq = seq // tq
    qblk = qmem_col // MEM_COLS
    assert qmem_col % MEM_COLS == 0
    return pl.pallas_call(
        _mem_kernel,
        grid=(batch, nq),
        in_specs=[
            pl.BlockSpec((tq, MEM_COLS), lambda b, i: (b * nq + i, qblk)),
            pl.BlockSpec((n_mem, 2 * MEM_COLS), lambda b, i: (b, 0)),
        ],
        out_specs=pl.BlockSpec((tq, MEM_COLS), lambda b, i: (b * nq + i, 0)),
        out_shape=jax.ShapeDtypeStruct((batch * seq, MEM_COLS), BF16),
        compiler_params=_cparams(("parallel", "arbitrary"), 32),
        name="mem_attention",
    )(proj, mem_kv)


def _matmul_norm_res_kernel(*refs, n_in, norm_chunk):
    a_refs, w_refs = refs[:n_in], refs[n_in:2 * n_in]
    g_ref, x_ref, o_ref, y_ref = refs[2 * n_in:]
    j = pl.program_id(1)
    tm, tn = x_ref.shape[0], w_refs[0].shape[1]

    y = _dot(a_refs[0][...], w_refs[0][...])
    for a_ref, w_ref in zip(a_refs[1:], w_refs[1:]):
        y = y + _dot(a_ref[...], w_ref[...])
    y_ref[:, pl.ds(pl.multiple_of(j * tn, tn), tn)] = y

    @pl.when(j == pl.num_programs(1) - 1)
    def _():
        def body(c, _):
            r = pl.multiple_of(c * norm_chunk, norm_chunk)
            rows = pl.ds(r, norm_chunk)
            o_ref[rows, :] = x_ref[rows, :] + _rms_rows(y_ref[rows, :], g_ref[...])
            return 0
        lax.fori_loop(0, tm // norm_chunk, body, 0)


def matmul_norm_residual(parts, w, layer, g, x, *, tm, tn):
    m, d = x.shape
    assert m % tm == 0 and d % tn == 0 and sum(p.shape[1] for p in parts) == w.shape[1]
    a_specs, w_specs, row0 = [], [], 0
    for p in parts:
        kp = p.shape[1]
        assert row0 % kp == 0
        a_specs.append(pl.BlockSpec((tm, kp), lambda i, j: (i, 0)))
        w_specs.append(pl.BlockSpec((None, kp, tn), lambda i, j, rb=row0 // kp: (layer, rb, j)))
        row0 += kp
    kern = functools.partial(_matmul_norm_res_kernel, n_in=len(parts), norm_chunk=min(tm, 128))
    return pl.pallas_call(
        kern,
        grid=(m // tm, d // tn),
        in_specs=a_specs + w_specs + [
            pl.BlockSpec((1, d), lambda i, j: (0, 0)),
            pl.BlockSpec((tm, d), lambda i, j: (i, 0)),
        ],
        out_specs=pl.BlockSpec((tm, d), lambda i, j: (i, 0)),
        out_shape=jax.ShapeDtypeStruct((m, d), F32),
        scratch_shapes=[pltpu.VMEM((tm, d), F32)],
        compiler_params=_cparams(("parallel", "arbitrary"), 56),
        name="matmul_norm_residual",
    )(*parts, *([w] * len(parts)), g.reshape(1, d), x)


def _ffn_up_kernel(x_ref, halo_ref, g_ref, wg_ref, wu_ref, cw_ref, cb_ref, o_ref, hn_ref, *, seq, norm_chunk):
    i, j = pl.program_id(0), pl.program_id(1)
    tm = x_ref.shape[0]

    @pl.when(j == 0)
    def _():
        def body(c, _):
            r = pl.multiple_of(c * norm_chunk, norm_chunk)
            hn_ref[pl.ds(r, norm_chunk), :] = _rms_rows(x_ref[pl.ds(r, norm_chunk), :], g_ref[...]).astype(BF16)
            return 0
        lax.fori_loop(0, tm // norm_chunk, body, 0)
        keep = jnp.where((i * tm) % seq == 0, 0.0, 1.0)
        hn_ref[tm:, :] = (_rms_rows(halo_ref[...], g_ref[...]) * keep).astype(BF16)

    hn = hn_ref[...]
    gate = _dot(hn, wg_ref[...])
    up = _dot(hn[:tm, :], wu_ref[...])
    g1 = pltpu.roll(gate, 1, 0)[:tm, :]
    g2 = pltpu.roll(gate, 2, 0)[:tm, :]
    gc = cb_ref[...] + g2 * cw_ref[0:1, :] + g1 * cw_ref[1:2, :] + gate[:tm, :] * cw_ref[2:3, :]
    o_ref[...] = ((gc * _sigmoid(gc)) * up).astype(o_ref.dtype)


def ffn_up(x, g, w_gate, w_up, layer, conv_w, conv_b, seq, *, tm, tf):
    m, d = x.shape
    f = w_gate.shape[2]
    assert m % tm == 0 and f % tf == 0 and seq % tm == 0 and tm % CONV_HALO == 0
    halo_blocks = tm // CONV_HALO
    kern = functools.partial(_ffn_up_kernel, seq=seq, norm_chunk=min(tm, 128))
    return pl.pallas_call(
        kern,
        grid=(m // tm, f // tf),
        in_specs=[
            pl.BlockSpec((tm, d), lambda i, j: (i, 0)),
            pl.BlockSpec((CONV_HALO, d), lambda i, j: (jnp.maximum(i * halo_blocks - 1, 0), 0)),
            pl.BlockSpec((1, d), lambda i, j: (0, 0)),
            pl.BlockSpec((None, d, tf), lambda i, j: (layer, 0, j)),
            pl.BlockSpec((None, d, tf), lambda i, j: (layer, 0, j)),
            pl.BlockSpec((CONV_WIDTH, tf), lambda i, j: (0, j)),
            pl.BlockSpec((1, tf), lambda i, j: (0, j)),
        ],
        out_specs=pl.BlockSpec((tm, tf), lambda i, j: (i, j)),
        out_shape=jax.ShapeDtypeStruct((m, f), BF16),
        scratch_shapes=[pltpu.VMEM((tm + CONV_HALO, d), BF16)],
        compiler_params=_cparams(("parallel", "arbitrary"), 56),
        name="ffn_up",
    )(x, x, g.reshape(1, d), w_gate, w_up, conv_w, conv_b.reshape(1, f))


def _repack_nsa_weight(w):
    q, kc, vc, ks, vs, kw, vw, gl, qm = jnp.split(
        w, [MIX_COLS + KV_COLS * n for n in range(7)] + [MIX_COLS + 6 * KV_COLS + N_BRANCH * N_MIX_HEADS], axis=1)
    per_group = N_BRANCH * HEADS_PER_GROUP
    pad = jnp.zeros((w.shape[0], LANES - per_group), w.dtype)
    gates = [jnp.concatenate([gl[:, g * per_group:(g + 1) * per_group], pad], axis=1) for g in range(N_KV_GROUPS)]
    return jnp.concatenate([q, kc, ks, kw, vc, vs, vw, qm] + gates, axis=1)


def _cmp_chunks(proj, batch, seq):
    n = seq // CMP_STRIDE
    pieces = [proj[:, (cb + g) * HEAD_DIM:(cb + g + 1) * HEAD_DIM].reshape(batch, n, CMP_STRIDE * HEAD_DIM)
              for cb in (_NSA_KC, _NSA_VC) for g in range(N_KV_GROUPS)]
    return jnp.stack(pieces).reshape(2, N_KV_GROUPS, batch, n, CMP_STRIDE * HEAD_DIM)


def kernel(x, mem, positions, norm_g, w_in_sb, w_in_nsa, cmp_pe, cmp_w1, cmp_w2, w_mem_kv, w_o,
           w_ffn_gate, w_ffn_up, ffn_conv_w, ffn_conv_b, w_ffn_down):
    batch, seq, d = x.shape
    n_mem = mem.shape[1]
    depth = norm_g.shape[0]
    m = batch * seq
    cosf, sinf = rope_tables(positions)
    xs = x.reshape(m, d)
    mem2 = mem.reshape(batch * n_mem, d)
    tm = 1024
    w_sb, w_kv, w_out = w_in_sb.astype(BF16), w_mem_kv.astype(BF16), w_o.astype(BF16)
    w_gate, w_up, w_down = w_ffn_gate.astype(BF16), w_ffn_up.astype(BF16), w_ffn_down.astype(BF16)

    for i in range(depth):
        j = i // 2
        if i % 2 == 0:
            tn = 512
            proj, v_t = norm_matmul(xs, norm_g[i, 0], w_sb, j, cosf, sinf, tm=tm, tn=tn,
                                    t_tiles=(2 * MIX_COLS // tn, MIX_COLS // tn))
            o_mix = sb_attention(proj, v_t, batch, seq, tq=256, heads=6)
            qmem_col = 3 * MIX_COLS
        else:
            tn = 768
            w_nsa = _repack_nsa_weight(w_in_nsa[j]).astype(BF16)[None]
            proj, proj_t = norm_matmul(xs, norm_g[i, 0], w_nsa, 0, cosf, sinf,
                                       tm=tm, tn=tn, rope_cols=NSA_ROPE_COLS,
                                       t_tiles=(_NSA_T_FIRST_COL * HEAD_DIM // tn, 2))
            cmp_kv = cmp_tokens(_cmp_chunks(proj, batch, seq), cmp_pe[j].reshape(2, 2, CMP_STRIDE * HEAD_DIM),
                                cmp_w1[j].astype(BF16), cmp_w2[j].astype(BF16))
            o_mix = nsa_attention(proj, proj_t, cmp_kv, batch, seq, tq=128, chunk=512)
            qmem_col = _NSA_QMEM_COL
        mem_kv = norm_matmul(mem2, norm_g[i, 4], w_kv, i, cosf, sinf, tm=n_mem, tn=1024)
        o_mem = mem_attention(proj, qmem_col, mem_kv, batch, seq, n_mem, tq=512)
        xs = matmul_norm_residual([o_mix, o_mem], w_out, i, norm_g[i, 1], xs, tm=512, tn=d)
        act = ffn_up(xs, norm_g[i, 2], w_gate, w_up, i, ffn_conv_w[i], ffn_conv_b[i], seq, tm=tm, tf=512)
        xs = matmul_norm_residual([act], w_down, i, norm_g[i, 3], xs, tm=512, tn=512)
    return xs.reshape(batch, seq, d)
```

```python
import functools
import math

import jax
import jax.numpy as jnp
from jax import lax
from jax.experimental import pallas as pl
from jax.experimental.pallas import tpu as pltpu

F32 = jnp.float32
BF16 = jnp.bfloat16

HEAD_DIM = 128
N_MIX_HEADS = 12
N_MEM_HEADS = 4
N_KV_GROUPS = 2
HEADS_PER_GROUP = N_MIX_HEADS // N_KV_GROUPS
N_BRANCH = 3
CMP_LEN = 32
CMP_STRIDE = 16
SEL_BLOCK = 64
N_SEL = 16
WINDOW = 512
CONV_WIDTH = 3
ROPE_THETA = 10000.0
NORM_EPS = 1e-6
ATTN_SCALE = HEAD_DIM ** -0.5
EXP2_SCALE = ATTN_SCALE * math.log2(math.e)
MASK_VALUE = -1e30

LANES = 128
MIX_COLS = N_MIX_HEADS * HEAD_DIM
MEM_COLS = N_MEM_HEADS * HEAD_DIM
KV_COLS = N_KV_GROUPS * HEAD_DIM
CONV_HALO = 16

_NT = (((1,), (1,)), ((), ()))


def _cparams(semantics, vmem_mib):
    return pltpu.CompilerParams(dimension_semantics=semantics,
                                vmem_limit_bytes=vmem_mib * 1024 * 1024)


def _dot(a, b):
    return jnp.dot(a, b, preferred_element_type=F32)


def _dot_nt(a, b):
    return lax.dot_general(a, b, _NT, preferred_element_type=F32)


def _split_bf16(x, axis):
    hi = x.astype(BF16)
    lo = (x - hi.astype(F32)).astype(BF16)
    return jnp.concatenate([hi, lo], axis=axis)


def _sigmoid(x):
    return 1.0 / (1.0 + jnp.exp(-x))


def _rms_rows(x, g):
    ms = jnp.mean(x * x, axis=-1, keepdims=True)
    return (x * lax.rsqrt(ms + NORM_EPS)) * g


def _rope_kernel(pos_ref, inv_ref, sign_ref, cos_ref, sin_ref):
    ang = pos_ref[...].astype(F32) * inv_ref[...]
    cos_ref[...] = jnp.cos(ang)
    sin_ref[...] = jnp.sin(ang) * sign_ref[...]


def rope_tables(positions):
    m = positions.size
    tm = min(m, 2048)
    inv_freq = jnp.power(ROPE_THETA, -jnp.arange(0, HEAD_DIM, 2, dtype=F32) / HEAD_DIM)
    inv_full = jnp.concatenate([inv_freq, inv_freq])[None, :]
    sign = jnp.concatenate([-jnp.ones((HEAD_DIM // 2,), F32), jnp.ones((HEAD_DIM // 2,), F32)])[None, :]
    row = pl.BlockSpec((tm, LANES), lambda i: (i, 0))
    const = pl.BlockSpec((1, LANES), lambda i: (0, 0))
    return pl.pallas_call(
        _rope_kernel,
        grid=(m // tm,),
        in_specs=[pl.BlockSpec((tm, 1), lambda i: (i, 0)), const, const],
        out_specs=[row, row],
        out_shape=[jax.ShapeDtypeStruct((m, LANES), F32)] * 2,
        compiler_params=_cparams(("parallel",), 32),
        name="rope_tables",
    )(positions.reshape(m, 1), inv_full, sign)


def _norm_matmul_kernel(x_ref, g_ref, w_ref, cos_ref, sin_ref, o_ref, *rest, rope_tiles, t_tiles, norm_chunk):
    hn_ref = rest[-1]
    j = pl.program_id(1)
    tm = x_ref.shape[0]

    @pl.when(j == 0)
    def _():
        def body(c, _):
            r = pl.multiple_of(c * norm_chunk, norm_chunk)
            hn_ref[pl.ds(r, norm_chunk), :] = _rms_rows(x_ref[pl.ds(r, norm_chunk), :], g_ref[...]).astype(BF16)
            return 0
        lax.fori_loop(0, tm // norm_chunk, body, 0)

    y = _dot(hn_ref[...], w_ref[...])

    if rope_tiles == 0:
        o_ref[...] = y.astype(o_ref.dtype)
    else:
        @pl.when(j < rope_tiles)
        def _():
            cosf, sinf = cos_ref[...], sin_ref[...]
            for c in range(y.shape[1] // HEAD_DIM):
                u = y[:, c * HEAD_DIM:(c + 1) * HEAD_DIM]
                o_ref[:, c * HEAD_DIM:(c + 1) * HEAD_DIM] = (
                    u * cosf + pltpu.roll(u, HEAD_DIM // 2, 1) * sinf).astype(o_ref.dtype)

        @pl.when(j >= rope_tiles)
        def _():
            o_ref[...] = y.astype(o_ref.dtype)

    if t_tiles is not None:
        first, count = t_tiles

        @pl.when((j >= first) & (j < first + count))
        def _():
            rest[0][...] = y.T.astype(rest[0].dtype)


def norm_matmul(x, g, w, layer, cosf, sinf, *, tm, tn, rope_cols=0, t_tiles=None):
    m, d = x.shape
    n = w.shape[2]
    assert m % tm == 0 and n % tn == 0 and rope_cols % tn == 0
    kern = functools.partial(_norm_matmul_kernel, rope_tiles=rope_cols // tn, t_tiles=t_tiles,
                             norm_chunk=min(tm, 128))
    out_specs = [pl.BlockSpec((tm, tn), lambda i, j: (i, j))]
    out_shape = [jax.ShapeDtypeStruct((m, n), BF16)]
    if t_tiles is not None:
        first, count = t_tiles
        out_specs.append(pl.BlockSpec((tn, tm), lambda i, j: (jnp.clip(j - first, 0, count - 1), i)))
        out_shape.append(jax.ShapeDtypeStruct((count * tn, m), BF16))
    outs = pl.pallas_call(
        kern,
        grid=(m // tm, n // tn),
        in_specs=[
            pl.BlockSpec((tm, d), lambda i, j: (i, 0)),
            pl.BlockSpec((1, d), lambda i, j: (0, 0)),
            pl.BlockSpec((None, d, tn), lambda i, j: (layer, 0, j)),
            pl.BlockSpec((tm, LANES), lambda i, j: (i, 0)),
            pl.BlockSpec((tm, LANES), lambda i, j: (i, 0)),
        ],
        out_specs=out_specs,
        out_shape=out_shape,
        scratch_shapes=[pltpu.VMEM((tm, d), BF16)],
        compiler_params=_cparams(("parallel", "arbitrary"), 56),
        name="norm_matmul",
    )(x, g.reshape(1, d), w, cosf, sinf)
    return outs if t_tiles is not None else outs[0]


def _sb_kernel(q_ref, k_ref, vt_ref, l_ref, o_ref, *, tq, heads):
    qi = pl.program_id(2)
    key = lax.broadcasted_iota(jnp.int32, (tq, tq), 0)
    qry = lax.broadcasted_iota(jnp.int32, (tq, tq), 1)
    causal = key < qry
    hs = range(heads)
    cols = [slice(h * HEAD_DIM, (h + 1) * HEAD_DIM) for h in hs]

    def scores(kb):
        start = pl.multiple_of(kb * tq, tq)
        return [_dot_nt(k_ref[pl.ds(start, tq), cols[h]], q_ref[:, cols[h]]) for h in hs]

    def blocks(raw, kb, cs, diagonal):
        start = pl.multiple_of(kb * tq, tq)
        z = [raw[h] * ATTN_SCALE for h in hs]
        e = [jnp.exp2(jnp.abs(raw[h]) * (-EXP2_SCALE)) for h in hs]
        log_beta = [jnp.minimum(z[h], 0.0) - jnp.log(1.0 + e[h]) for h in hs]
        log_keep = [log_beta[h] - z[h] for h in hs]
        if diagonal:
            log_keep = [jnp.where(causal, log_keep[h], 0.0) for h in hs]
        later = [_dot(l_ref[...], log_keep[h].astype(BF16)) for h in hs]
        a = [jnp.exp(log_beta[h] + later[h] + cs[h]) for h in hs]
        if diagonal:
            a = [jnp.where(causal, a[h], 0.0) for h in hs]
        pv = [_dot(vt_ref[cols[h], pl.ds(start, tq)], a[h].astype(BF16)) for h in hs]
        return pv, [cs[h] + later[h][0:1, :] + log_keep[h][0:1, :] for h in hs]

    acc, cs = blocks(scores(qi), qi, [jnp.zeros((1, tq), F32)] * heads, True)

    def body(i, carry):
        acc, cs = carry
        pv, cs = blocks(scores(qi - i), qi - i, cs, False)
        return [acc[h] + pv[h] for h in hs], cs

    acc, _ = lax.fori_loop(1, qi + 1, body, (acc, cs))
    for h in hs:
        o_ref[:, cols[h]] = acc[h].T.astype(o_ref.dtype)


def sb_attention(proj, v_t, batch, seq, *, tq, heads):
    nq = seq // tq
    hb = N_MIX_HEADS // heads
    assert N_MIX_HEADS % heads == 0
    width = heads * HEAD_DIM
    s = lax.broadcasted_iota(jnp.int32, (tq, tq), 0)
    j = lax.broadcasted_iota(jnp.int32, (tq, tq), 1)
    later = (j > s).astype(BF16)
    return pl.pallas_call(
        functools.partial(_sb_kernel, tq=tq, heads=heads),
        grid=(batch, hb, nq),
        in_specs=[
            pl.BlockSpec((tq, width), lambda b, hh, i: (b * nq + i, hh)),
            pl.BlockSpec((seq, width), lambda b, hh, i: (b, hb + hh)),
            pl.BlockSpec((width, seq), lambda b, hh, i: (hh, b)),
            pl.BlockSpec((tq, tq), lambda b, hh, i: (0, 0)),
        ],
        out_specs=pl.BlockSpec((tq, width), lambda b, hh, i: (b * nq + i, hh)),
        out_shape=jax.ShapeDtypeStruct((batch * seq, MIX_COLS), BF16),
        compiler_params=_cparams(("parallel", "parallel", "arbitrary"), 32),
        name="sb_attention",
    )(proj, proj, v_t, later)


def _cmp_kernel(x_ref, pe_ref, w1_ref, w2_ref, o_ref):
    half = CMP_STRIDE * HEAD_DIM
    x = x_ref[...].astype(F32)
    pe = pe_ref[...]
    first = _dot((x + pe[0:1, :]).astype(BF16), w1_ref[:half, :])
    second = _dot((x + pe[1:2, :]).astype(BF16), w1_ref[half:, :])
    n = x.shape[0]
    hid = first + pltpu.roll(second, n - 1, 0)
    hid = hid * _sigmoid(hid)
    o_ref[...] = _dot(hid.astype(BF16), w2_ref[...]).astype(o_ref.dtype)


def cmp_tokens(chunks, pe, w1, w2):
    _, g, b, n, half = chunks.shape
    hid = w1.shape[-1]
    return pl.pallas_call(
        _cmp_kernel,
        grid=(2, b, g),
        in_specs=[
            pl.BlockSpec((None, None, None, n, half), lambda kv, bb, gg: (kv, gg, bb, 0, 0)),
            pl.BlockSpec((None, 2, half), lambda kv, bb, gg: (kv, 0, 0)),
            pl.BlockSpec((None, 2 * half, hid), lambda kv, bb, gg: (kv, 0, 0)),
            pl.BlockSpec((None, hid, HEAD_DIM), lambda kv, bb, gg: (kv, 0, 0)),
        ],
        out_specs=pl.BlockSpec((None, None, None, n, HEAD_DIM), lambda kv, bb, gg: (kv, bb, gg, 0, 0)),
        out_shape=jax.ShapeDtypeStruct((2, b, g, n, HEAD_DIM), BF16),
        compiler_params=_cparams(("parallel", "parallel", "parallel"), 32),
        name="cmp_tokens",
    )(chunks, pe, w1, w2)


def _nsa_kernel(q_ref, kc_ref, vct_ref, ks_ref, vst_ref, kw_ref, vwt_ref, glt_ref, ovt_ref,
                o_ref, ocmp_ref, owin_ref, selb_ref, *, tq, n_cmp, n_sel_blocks, chunk):
    qi = pl.program_id(2)
    hpg = HEADS_PER_GROUP
    t0 = qi * tq
    tok = lax.broadcasted_iota(jnp.int32, (LANES, tq), 1) + t0
    sub = lax.broadcasted_iota(jnp.int32, (LANES, tq), 0)
    qs =jnp.concatenate([q_ref[:, h * HEAD_DIM:(h + 1) * HEAD_DIM] for h in range(hpg)], axis=0)

    def per_head(x):
        return jnp.concatenate([x] * hpg, axis=1)

    width = hpg * tq

    def key_mask(start, size, lower):
        ktok = lax.broadcasted_iota(jnp.int32, (size, tq), 0) + start
        qtok = lax.broadcasted_iota(jnp.int32, (size, tq), 1) + t0
        ok = ktok <= qtok
        if lower:
            ok = ok & (ktok > qtok - WINDOW)
        return per_head(jnp.where(ok, 0.0, MASK_VALUE))

    wsize = WINDOW + tq
    ws = pl.multiple_of(jnp.maximum(t0 - WINDOW, 0), tq)
    st_w = _dot_nt(kw_ref[pl.ds(ws, wsize), :], qs) + key_mask(ws, wsize, True)

    c_ok = (sub * CMP_STRIDE + (CMP_LEN - 1) <= tok) & (sub < n_cmp)
    s = _dot_nt(kc_ref[...], qs) * ATTN_SCALE + per_head(jnp.where(c_ok, 0.0, MASK_VALUE))
    e = jnp.exp(s - jnp.max(s, axis=0, keepdims=True))
    tok_row = lax.broadcasted_iota(jnp.int32, (1, tq), 1) + t0
    has_block = per_head(jnp.where(tok_row >= CMP_LEN - 1, 1.0, 0.0))
    p = e * (has_block / jnp.sum(e, axis=0, keepdims=True))
    ocmp_ref[...] = _dot(vct_ref[...], p.astype(BF16))
    p_sum = p[:, 0:tq]
    for h in range(1, hpg):
        p_sum = p_sum + p[:, h * tq:(h + 1) * tq]

    imp = _dot(ovt_ref[...], _split_bf16(p_sum, 0))

    p_w = jnp.exp2((st_w - jnp.max(st_w, axis=0, keepdims=True)) * EXP2_SCALE)
    l_w = jnp.sum(p_w, axis=0, keepdims=True)
    owin_ref[...] = _dot(vwt_ref[:, pl.ds(ws, wsize)], p_w.astype(BF16)) * (1.0 / l_w)

    nsb = n_sel_blocks
    jrow = lax.broadcasted_iota(jnp.int32, (nsb, tq), 0)
    jtok = lax.broadcasted_iota(jnp.int32, (nsb, tq), 1) + t0
    cur = jnp.right_shift(jtok, SEL_BLOCK.bit_length() - 1)
    valid = jrow <= cur
    forced = (jrow == 0) | (jrow == cur) | (jrow == cur - 1)
    score = jnp.where(forced, jnp.inf, jnp.where(valid, imp[:nsb, :], -jnp.inf))
    rank = jnp.zeros((nsb, tq), jnp.int32)
    for jp in range(nsb):
        other = score[jp:jp + 1, :]
        ahead = (other > score) | ((other == score) & (jrow > jp))
        rank = rank + jnp.where(ahead, 1, 0)
    chosen = jnp.where((rank < N_SEL) & valid, 0.0, MASK_VALUE)
    selb_ref[...] = per_head(chosen)

    per_chunk = chunk // SEL_BLOCK

    def sel_scores(c):
        start = pl.multiple_of(c * chunk, chunk)
        st = _dot_nt(ks_ref[pl.ds(start, chunk), :], qs)
        rows = [jnp.broadcast_to(selb_ref[pl.ds(c * per_chunk + r, 1), :], (SEL_BLOCK, width))
                for r in range(per_chunk)]
        return st + jnp.concatenate(rows, axis=0), start

    def accumulate(carry, st, start):
        m, l, acc = carry
        m_new = jnp.maximum(m, jnp.max(st, axis=0, keepdims=True))
        p = jnp.exp2((st - m_new) * EXP2_SCALE)
        alpha = jnp.exp2((m - m_new) * EXP2_SCALE)
        l = alpha * l + jnp.sum(p, axis=0, keepdims=True)
        acc = alpha * acc + _dot(vst_ref[:, pl.ds(start, chunk)], p.astype(BF16))
        return m_new, l, acc

    def sel_body(c, carry):
        st, start = sel_scores(c)
        return accumulate(carry, st, start)

    last = qi // (chunk // tq)
    carry = (jnp.full((1, width), MASK_VALUE, F32), jnp.zeros((1, width), F32),
             jnp.zeros((HEAD_DIM, width), F32))
    carry = lax.fori_loop(0, last, sel_body, carry)
    st, start = sel_scores(last)
    _, l, acc = accumulate(carry, st + key_mask(start, chunk, False), start)
    o_sel = acc * (1.0 / l)

    gate =_sigmoid(glt_ref[...].astype(F32))
    for h in range(hpg):
        cols = slice(h * tq, (h + 1) * tq)
        o = (gate[3 * h:3 * h + 1, :] * ocmp_ref[:, cols]
             + gate[3 * h + 1:3 * h + 2, :] * o_sel[:, cols]
             + gate[3 * h + 2:3 * h + 3, :] * owin_ref[:, cols])
        o_ref[:, h * HEAD_DIM:(h + 1) * HEAD_DIM] = o.T.astype(o_ref.dtype)


_NSA_KC, _NSA_KS, _NSA_KW = 12, 14, 16
_NSA_VC, _NSA_VS = 18, 20
_NSA_QMEM_COL = 24 * HEAD_DIM
_NSA_GATE = 28
NSA_PROJ_COLS = 30 * HEAD_DIM
NSA_ROPE_COLS = 18 * HEAD_DIM


_NSA_T_FIRST_COL = _NSA_VC
_NSA_T_VS = _NSA_VS - _NSA_T_FIRST_COL
_NSA_T_VW = _NSA_T_VS + N_KV_GROUPS
_NSA_T_GATE = _NSA_GATE - _NSA_T_FIRST_COL


def nsa_attention(proj, proj_t, cmp_kv, batch, seq, *, tq, chunk):
    nq = seq // tq
    n_chunks = seq // CMP_STRIDE
    n_cmp = n_chunks - 1
    ns = seq // SEL_BLOCK
    assert n_chunks == LANES and ns <= LANES and ns % 8 == 0 and WINDOW % tq == 0 and tq == LANES
    assert seq % chunk == 0 and chunk % tq == 0 and chunk % SEL_BLOCK == 0 and seq >= WINDOW + tq
    c0 = jnp.arange(LANES)[None, :] * CMP_STRIDE
    s0 = jnp.arange(LANES)[:, None] * SEL_BLOCK
    ovt = jnp.maximum(jnp.minimum(c0 + CMP_LEN, s0 + SEL_BLOCK) - jnp.maximum(c0, s0), 0)
    ovt = (ovt.astype(F32) / CMP_STRIDE).astype(BF16)
    ovt2 = jnp.concatenate([ovt, ovt], axis=1)
    vct = jnp.swapaxes(cmp_kv[1], -1, -2)

    def k_spec(col0):
        return pl.BlockSpec((seq, HEAD_DIM), lambda b, g, i: (b, col0 + g))

    def vt_spec(row0):
        return pl.BlockSpec((HEAD_DIM, seq), lambda b, g, i: (row0 + g, b))

    return pl.pallas_call(
        functools.partial(_nsa_kernel, tq=tq, n_cmp=n_cmp, n_sel_blocks=ns, chunk=chunk),
        grid=(batch, N_KV_GROUPS, nq),
        in_specs=[
            pl.BlockSpec((tq, HEADS_PER_GROUP * HEAD_DIM), lambda b, g, i: (b * nq + i, g)),
            pl.BlockSpec((None, None, None, n_chunks, HEAD_DIM), lambda b, g, i: (0, b, g, 0, 0)),
            pl.BlockSpec((None, None, HEAD_DIM, n_chunks), lambda b, g, i: (b, g, 0, 0)),
            k_spec(_NSA_KS), vt_spec(_NSA_T_VS), k_spec(_NSA_KW), vt_spec(_NSA_T_VW),
            pl.BlockSpec((LANES, tq), lambda b, g, i: (_NSA_T_GATE + g, b * nq + i)),
            pl.BlockSpec((LANES, 2 * LANES), lambda b, g, i: (0, 0)),
        ],
        out_specs=pl.BlockSpec((tq, HEADS_PER_GROUP * HEAD_DIM), lambda b, g, i: (b * nq + i, g)),
        out_shape=jax.ShapeDtypeStruct((batch * seq, MIX_COLS), BF16),
        scratch_shapes=[pltpu.VMEM((HEAD_DIM, HEADS_PER_GROUP * tq), F32),
                        pltpu.VMEM((HEAD_DIM, HEADS_PER_GROUP * tq), F32),
                        pltpu.VMEM((ns, HEADS_PER_GROUP * tq), F32)],
        compiler_params=_cparams(("parallel", "parallel", "arbitrary"), 48),
        name="nsa_attention",
    )(proj, cmp_kv, vct, proj, proj_t, proj, proj_t, proj_t, ovt2)


def _mem_kernel(q_ref, kv_ref, o_ref):
    for h in range(N_MEM_HEADS):
        cols = slice(h * HEAD_DIM, (h + 1) * HEAD_DIM)
        k = kv_ref[:, cols]
        v = kv_ref[:, MEM_COLS + h * HEAD_DIM:MEM_COLS + (h + 1) * HEAD_DIM]
        s = _dot_nt(q_ref[:, cols], k) * ATTN_SCALE
        e = jnp.exp(s - jnp.max(s, axis=-1, keepdims=True))
        p = e / jnp.sum(e, axis=-1, keepdims=True)
        o_ref[:, cols] = _dot(p.astype(BF16), v).astype(o_ref.dtype)


def mem_attention(proj, qmem_col, mem_kv, batch, seq, n_mem, *, tq):
    nq = seq // tq
    qblk = qmem_col // MEM_COLS
    assert qmem_col % MEM_COLS == 0
    return pl.pallas_call(
        _mem_kernel,
        grid=(batch, nq),
        in_specs=[
            pl.BlockSpec((tq, MEM_COLS), lambda b, i: (b * nq + i, qblk)),
            pl.BlockSpec((n_mem, 2 * MEM_COLS), lambda b, i: (b, 0)),
        ],
        out_specs=pl.BlockSpec((tq, MEM_COLS), lambda b, i: (b * nq + i, 0)),
        out_shape=jax.ShapeDtypeStruct((batch * seq, MEM_COLS), BF16),
        compiler_params=_cparams(("parallel", "arbitrary"), 32),
        name="mem_attention",
    )(proj, mem_kv)


def _matmul_norm_res_kernel(*refs, n_in, norm_chunk):
    a_refs, w_refs = refs[:n_in], refs[n_in:2 * n_in]
    g_ref, x_ref, o_ref, y_ref = refs[2 * n_in:]
    j = pl.program_id(1)
    tm, tn = x_ref.shape[0], w_refs[0].shape[1]

    y = _dot(a_refs[0][...], w_refs[0][...])
    for a_ref, w_ref in zip(a_refs[1:], w_refs[1:]):
        y = y + _dot(a_ref[...], w_ref[...])
    y_ref[:, pl.ds(pl.multiple_of(j * tn, tn), tn)] = y

    @pl.when(j == pl.num_programs(1) - 1)
    def _():
        def body(c, _):
            r = pl.multiple_of(c * norm_chunk, norm_chunk)
            rows = pl.ds(r, norm_chunk)
            o_ref[rows, :] = x_ref[rows, :] + _rms_rows(y_ref[rows, :], g_ref[...])
            return 0
        lax.fori_loop(0, tm // norm_chunk, body, 0)


def matmul_norm_residual(parts, w, layer, g, x, *, tm, tn):
    m, d = x.shape
    assert m % tm == 0 and d % tn == 0 and sum(p.shape[1] for p in parts) == w.shape[1]
    a_specs, w_specs, row0 = [], [], 0
    for p in parts:
        kp = p.shape[1]
        assert row0 % kp == 0
        a_specs.append(pl.BlockSpec((tm, kp), lambda i, j: (i, 0)))
        w_specs.append(pl.BlockSpec((None, kp, tn), lambda i, j, rb=row0 // kp: (layer, rb, j)))
        row0 += kp
    kern = functools.partial(_matmul_norm_res_kernel, n_in=len(parts), norm_chunk=min(tm, 128))
    return pl.pallas_call(
        kern,
        grid=(m // tm, d // tn),
        in_specs=a_specs + w_specs + [
            pl.BlockSpec((1, d), lambda i, j: (0, 0)),
            pl.BlockSpec((tm, d), lambda i, j: (i, 0)),
        ],
        out_specs=pl.BlockSpec((tm, d), lambda i, j: (i, 0)),
        out_shape=jax.ShapeDtypeStruct((m, d), F32),
        scratch_shapes=[pltpu.VMEM((tm, d), F32)],
        compiler_params=_cparams(("parallel", "arbitrary"), 56),
        name="matmul_norm_residual",
    )(*parts, *([w] * len(parts)), g.reshape(1, d), x)


def _ffn_up_kernel(x_ref, halo_ref, g_ref, wg_ref, wu_ref, cw_ref, cb_ref, o_ref, hn_ref, *, seq, norm_chunk):
    i, j = pl.program_id(0), pl.program_id(1)
    tm = x_ref.shape[0]

    @pl.when(j == 0)
    def _():
        def body(c, _):
            r = pl.multiple_of(c * norm_chunk, norm_chunk)
            hn_ref[pl.ds(r, norm_chunk), :] = _rms_rows(x_ref[pl.ds(r, norm_chunk), :], g_ref[...]).astype(BF16)
            return 0
        lax.fori_loop(0, tm // norm_chunk, body, 0)
        keep = jnp.where((i * tm) % seq == 0, 0.0, 1.0)
        hn_ref[tm:, :] = (_rms_rows(halo_ref[...], g_ref[...]) * keep).astype(BF16)

    hn = hn_ref[...]
    gate = _dot(hn, wg_ref[...])
    up = _dot(hn[:tm, :], wu_ref[...])
    g1 = pltpu.roll(gate, 1, 0)[:tm, :]
    g2 = pltpu.roll(gate, 2, 0)[:tm, :]
    gc = cb_ref[...] + g2 * cw_ref[0:1, :] + g1 * cw_ref[1:2, :] + gate[:tm, :] * cw_ref[2:3, :]
    o_ref[...] = ((gc * _sigmoid(gc)) * up).astype(o_ref.dtype)


def ffn_up(x, g, w_gate, w_up, layer, conv_w, conv_b, seq, *, tm, tf):
    m, d = x.shape
    f = w_gate.shape[2]
    assert m % tm == 0 and f % tf == 0 and seq % tm == 0 and tm % CONV_HALO == 0
    halo_blocks = tm // CONV_HALO
    kern = functools.partial(_ffn_up_kernel, seq=seq, norm_chunk=min(tm, 128))
    return pl.pallas_call(
        kern,
        grid=(m // tm, f // tf),
        in_specs=[
            pl.BlockSpec((tm, d), lambda i, j: (i, 0)),
            pl.BlockSpec((CONV_HALO, d), lambda i, j: (jnp.maximum(i * halo_blocks - 1, 0), 0)),
            pl.BlockSpec((1, d), lambda i, j: (0, 0)),
            pl.BlockSpec((None, d, tf), lambda i, j: (layer, 0, j)),
            pl.BlockSpec((None, d, tf), lambda i, j: (layer, 0, j)),
            pl.BlockSpec((CONV_WIDTH, tf), lambda i, j: (0, j)),
            pl.BlockSpec((1, tf), lambda i, j: (0, j)),
        ],
        out_specs=pl.BlockSpec((tm, tf), lambda i, j: (i, j)),
        out_shape=jax.ShapeDtypeStruct((m, f), BF16),
        scratch_shapes=[pltpu.VMEM((tm + CONV_HALO, d), BF16)],
        compiler_params=_cparams(("parallel", "arbitrary"), 56),
        name="ffn_up",
    )(x, x, g.reshape(1, d), w_gate, w_up, conv_w, conv_b.reshape(1, f))


def _repack_nsa_weight(w):
    q, kc, vc, ks, vs, kw, vw, gl, qm = jnp.split(
        w, [MIX_COLS + KV_COLS * n for n in range(7)] + [MIX_COLS + 6 * KV_COLS + N_BRANCH * N_MIX_HEADS], axis=1)
    per_group = N_BRANCH * HEADS_PER_GROUP
    pad = jnp.zeros((w.shape[0], LANES - per_group), w.dtype)
    gates = [jnp.concatenate([gl[:, g * per_group:(g + 1) * per_group], pad], axis=1) for g in range(N_KV_GROUPS)]
    return jnp.concatenate([q, kc, ks, kw, vc, vs, vw, qm] + gates, axis=1)


def _cmp_chunks(proj, batch, seq):
    n = seq // CMP_STRIDE
    pieces = [proj[:, (cb + g) * HEAD_DIM:(cb + g + 1) * HEAD_DIM].reshape(batch, n, CMP_STRIDE * HEAD_DIM)
              for cb in (_NSA_KC, _NSA_VC) for g in range(N_KV_GROUPS)]
    return jnp.stack(pieces).reshape(2, N_KV_GROUPS, batch, n, CMP_STRIDE * HEAD_DIM)


def kernel(x, mem, positions, norm_g, w_in_sb, w_in_nsa, cmp_pe, cmp_w1, cmp_w2, w_mem_kv, w_o,
           w_ffn_gate, w_ffn_up, ffn_conv_w, ffn_conv_b, w_ffn_down):
    batch, seq, d = x.shape
    n_mem = mem.shape[1]
    depth = norm_g.shape[0]
    m = batch * seq
    cosf, sinf = rope_tables(positions)
    xs = x.reshape(m, d)
    mem2 = mem.reshape(batch * n_mem, d)
    tm = 1024
    w_sb, w_kv, w_out = w_in_sb.astype(BF16), w_mem_kv.astype(BF16), w_o.astype(BF16)
    w_gate, w_up, w_down = w_ffn_gate.astype(BF16), w_ffn_up.astype(BF16), w_ffn_down.astype(BF16)

    for i in range(depth):
        j = i // 2
        if i % 2 == 0:
            tn = 1024
            proj, v_t = norm_matmul(xs, norm_g[i, 0], w_sb, j, cosf, sinf, tm=tm, tn=tn,
                                    t_tiles=(2 * MIX_COLS // tn, pl.cdiv(MIX_COLS, tn)))
            o_mix = sb_attention(proj, v_t, batch, seq, tq=256, heads=6)
            qmem_col = 3 * MIX_COLS
        else:
            tn = 768
            w_nsa = _repack_nsa_weight(w_in_nsa[j]).astype(BF16)[None]
            proj, proj_t = norm_matmul(xs, norm_g[i, 0], w_nsa, 0, cosf, sinf,
                                       tm=tm, tn=tn, rope_cols=NSA_ROPE_COLS,
                                       t_tiles=(_NSA_T_FIRST_COL * HEAD_DIM // tn, 2))
            cmp_kv = cmp_tokens(_cmp_chunks(proj, batch, seq), cmp_pe[j].reshape(2, 2, CMP_STRIDE * HEAD_DIM),
                                cmp_w1[j].astype(BF16), cmp_w2[j].astype(BF16))
            o_mix = nsa_attention(proj, proj_t, cmp_kv, batch, seq, tq=128, chunk=512)
            qmem_col = _NSA_QMEM_COL
        mem_kv = norm_matmul(mem2, norm_g[i, 4], w_kv, i, cosf, sinf, tm=n_mem, tn=1024)
        o_mem = mem_attention(proj, qmem_col, mem_kv, batch, seq, n_mem, tq=512)
        xs = matmul_norm_residual([o_mix, o_mem], w_out, i, norm_g[i, 1], xs, tm=512, tn=d)
        act = ffn_up(xs, norm_g[i, 2], w_gate, w_up, i, ffn_conv_w[i], ffn_conv_b[i], seq, tm=tm, tf=512)
        xs = matmul_norm_residual([act], w_down, i, norm_g[i, 3], xs, tm=512, tn=512)
    return xs.reshape(batch, seq, d)
```
